```python
import math
import jax, jax.numpy as jnp
from jax import lax
import numpy as np

D_MODEL = 1024
BATCH = 32
SEQ = 2048
DEPTH = 2
DEC_BATCH = 4
DEC_SEQ = 4096
PAST_LEN = 128

GRID_W = 64
HEAD_DIM = 64
AXIAL_DIM = HEAD_DIM // 2
GQA_HEADS = 8
GQA_KV_HEADS = 2
GQA_GROUP = GQA_HEADS // GQA_KV_HEADS
DIFF_HEADS = 4
DIFF_V_DIM = 2 * HEAD_DIM
GQA_Q = GQA_HEADS * HEAD_DIM
GQA_KV = GQA_KV_HEADS * HEAD_DIM
DIFF_QK = DIFF_HEADS * 2 * HEAD_DIM
DIFF_V = DIFF_HEADS * DIFF_V_DIM
MIX_WIDTH = GQA_Q + DIFF_V
IN_WIDTH = GQA_Q + 2 * GQA_KV + 2 * DIFF_QK + DIFF_V
SPLITS = (GQA_Q, GQA_Q + GQA_KV, GQA_Q + 2 * GQA_KV, GQA_Q + 2 * GQA_KV + DIFF_QK, GQA_Q + 2 * GQA_KV + 2 * DIFF_QK)
D_FF = ((8 * D_MODEL + 3 * 256 - 1) // (3 * 256)) * 256
ROPE_THETA = 10000.0
Q_BLOCK = 128
NORM_EPS = 1e-6
DIFF_NORM_EPS = 1e-5

kernel_name = "hybrid_gqa_axial_diffattn_encoder"


def rms_norm(x, g, eps):
    xf = x.astype(jnp.float32)
    y = xf * lax.rsqrt(jnp.mean(xf * xf, axis=-1, keepdims=True) + eps)
    return (y * g.astype(jnp.float32)).astype(x.dtype)


def rope_angles(pos, dim):
    inv = ROPE_THETA ** (-jnp.arange(0, dim, 2, dtype=jnp.float32) / dim)
    ang = pos.astype(jnp.float32)[:, None] * inv[None, :]
    return jnp.cos(ang), jnp.sin(ang)


def apply_rope(x, cos, sin):
    half = x.shape[-1] // 2
    shp = (1, x.shape[1]) + (1,) * (x.ndim - 3) + (half,)
    c = cos.reshape(shp).astype(x.dtype)
    s = sin.reshape(shp).astype(x.dtype)
    x1, x2 = x[..., :half], x[..., half:]
    return jnp.concatenate([x1 * c - x2 * s, x1 * s + x2 * c], axis=-1)


def to_blocks(a):
    B, S = a.shape[:2]
    return jnp.moveaxis(a.reshape((B, S // Q_BLOCK, Q_BLOCK) + a.shape[2:]), 1, 0)


def from_blocks(o):
    o = jnp.moveaxis(o, 0, 1)
    return o.reshape((o.shape[0], o.shape[1] * o.shape[2]) + o.shape[3:])


def gqa_attention(q, k, v):
    scale = HEAD_DIM ** -0.5

    def block(qb):
        s = jnp.einsum('bqkgd,bskd->bkgqs', qb, k).astype(jnp.float32) * scale
        p = jax.nn.softmax(s, axis=-1).astype(v.dtype)
        return jnp.einsum('bkgqs,bskd->bqkgd', p, v)

    return from_blocks(lax.map(block, to_blocks(q)))


def diff_attention(q1, q2, k1, k2, v, lam):
    scale = HEAD_DIM ** -0.5

    def block(qs):
        q1b, q2b = qs
        s1 = jnp.einsum('bqhd,bshd->bhqs', q1b, k1).astype(jnp.float32) * scale
        s2 = jnp.einsum('bqhd,bshd->bhqs', q2b, k2).astype(jnp.float32) * scale
        p = (jax.nn.softmax(s1, axis=-1) - lam * jax.nn.softmax(s2, axis=-1)).astype(v.dtype)
        return jnp.einsum('bhqs,bshe->bqhe', p, v)

    return from_blocks(lax.map(block, (to_blocks(q1), to_blocks(q2))))


def encoder_layer(x, layer_idx, axial_cs, seq_cs, w_in, w_out, attn_norm, gqa_q_norm, gqa_k_norm,
                  lq1, lk1, lq2, lk2, diff_sub_norm, ffn_norm, w_gate_up, w_down):
    B, S, _ = x.shape
    (row_c, row_s, col_c, col_s) = axial_cs
    (t_c, t_s) = seq_cs
    h = rms_norm(x, attn_norm, NORM_EPS)
    proj = h @ w_in
    q_g, k_g, v_g, q_d, k_d, v_d = jnp.split(proj, SPLITS, axis=-1)

    q_g = rms_norm(q_g.reshape(B, S, GQA_HEADS, HEAD_DIM), gqa_q_norm, NORM_EPS)
    k_g = rms_norm(k_g.reshape(B, S, GQA_KV_HEADS, HEAD_DIM), gqa_k_norm, NORM_EPS)
    v_g = v_g.reshape(B, S, GQA_KV_HEADS, HEAD_DIM)

    def axial(a):
        return jnp.concatenate([apply_rope(a[..., :AXIAL_DIM], row_c, row_s),
                                apply_rope(a[..., AXIAL_DIM:], col_c, col_s)], axis=-1)

    q_g = axial(q_g).reshape(B, S, GQA_KV_HEADS, GQA_GROUP, HEAD_DIM)
    k_g = axial(k_g)
    o_g = gqa_attention(q_g, k_g, v_g).reshape(B, S, GQA_Q)

    q_d = apply_rope(q_d.reshape(B, S, DIFF_HEADS, 2, HEAD_DIM), t_c, t_s)
    k_d = apply_rope(k_d.reshape(B, S, DIFF_HEADS, 2, HEAD_DIM), t_c, t_s)
    v_d = v_d.reshape(B, S, DIFF_HEADS, DIFF_V_DIM)
    lambda_init = 0.8 - 0.6 * math.exp(-0.3 * layer_idx)
    lam = (jnp.exp(jnp.sum(lq1.astype(jnp.float32) * lk1.astype(jnp.float32)))
           - jnp.exp(jnp.sum(lq2.astype(jnp.float32) * lk2.astype(jnp.float32))) + lambda_init)
    o_d = diff_attention(q_d[..., 0, :], q_d[..., 1, :], k_d[..., 0, :], k_d[..., 1, :], v_d, lam)
    o_d = (rms_norm(o_d, diff_sub_norm, DIFF_NORM_EPS) * (1.0 - lambda_init)).reshape(B, S, DIFF_V)

    x = x + jnp.concatenate([o_g, o_d], axis=-1) @ w_out

    h = rms_norm(x, ffn_norm, NORM_EPS)
    gate, up = jnp.split(h @ w_gate_up, 2, axis=-1)
    return x + (jax.nn.silu(gate) * up) @ w_down


def encoder_trunk(x, w_in, w_out, attn_norm, gqa_q_norm, gqa_k_norm, diff_lambda_q1, diff_lambda_k1,
                  diff_lambda_q2, diff_lambda_k2, diff_sub_norm, ffn_norm, w_gate_up, w_down, final_norm):
    S = x.shape[1]
    rows = S // GRID_W
    row = jnp.repeat(jnp.arange(rows, dtype=jnp.int32), GRID_W)
    col = jnp.tile(jnp.arange(GRID_W, dtype=jnp.int32), rows)
    t = jnp.arange(S, dtype=jnp.int32)
    row_c, row_s = rope_angles(row, AXIAL_DIM)
    col_c, col_s = rope_angles(col, AXIAL_DIM)
    t_c, t_s = rope_angles(t, HEAD_DIM)
    for l in range(DEPTH):
        x = encoder_layer(x, l, (row_c, row_s, col_c, col_s), (t_c, t_s), w_in[l], w_out[l], attn_norm[l],
                          gqa_q_norm[l], gqa_k_norm[l], diff_lambda_q1[l], diff_lambda_k1[l],
                          diff_lambda_q2[l], diff_lambda_k2[l], diff_sub_norm[l], ffn_norm[l],
                          w_gate_up[l], w_down[l])
    return rms_norm(x, final_norm, NORM_EPS)


def setup_inputs(seed: int = 0) -> dict:
    key = jax.random.key(seed)
    ks = jax.random.split(key, 18)
    f32 = jnp.float32

    def gain(k, shape):
        return 1.0 + 0.02 * jax.random.normal(k, shape, f32)

    return {
        "x_prompt": jax.random.normal(ks[0], (BATCH, SEQ, D_MODEL), f32),
        "x_sample": jax.random.normal(ks[1], (DEC_BATCH, DEC_SEQ, D_MODEL), f32),
        "w_in": jax.random.normal(ks[2], (DEPTH, D_MODEL, IN_WIDTH), f32) * D_MODEL ** -0.5,
        "w_out": jax.random.normal(ks[3], (DEPTH, MIX_WIDTH, D_MODEL), f32) * MIX_WIDTH ** -0.5,
        "attn_norm": gain(ks[4], (DEPTH, D_MODEL)),
        "gqa_q_norm": gain(ks[5], (DEPTH, HEAD_DIM)),
        "gqa_k_norm": gain(ks[6], (DEPTH, HEAD_DIM)),
        "diff_lambda_q1": 0.1 * jax.random.normal(ks[7], (DEPTH, HEAD_DIM), f32),
        "diff_lambda_k1": 0.1 * jax.random.normal(ks[8], (DEPTH, HEAD_DIM), f32),
        "diff_lambda_q2": 0.1 * jax.random.normal(ks[9], (DEPTH, HEAD_DIM), f32),
        "diff_lambda_k2": 0.1 * jax.random.normal(ks[10], (DEPTH, HEAD_DIM), f32),
        "diff_sub_norm": gain(ks[11], (DEPTH, DIFF_V_DIM)),
        "ffn_norm": gain(ks[12], (DEPTH, D_MODEL)),
        "w_gate_up": jax.random.normal(ks[13], (DEPTH, D_MODEL, 2 * D_FF), f32) * D_MODEL ** -0.5,
        "w_down": jax.random.normal(ks[14], (DEPTH, D_FF, D_MODEL), f32) * D_FF ** -0.5,
        "final_norm": gain(ks[15], (D_MODEL,)),
    }


def reference(x_prompt, x_sample, w_in, w_out, attn_norm, gqa_q_norm, gqa_k_norm, diff_lambda_q1,
              diff_lambda_k1, diff_lambda_q2, diff_lambda_k2, diff_sub_norm, ffn_norm, w_gate_up, w_down,
              final_norm):
    y_prompt = encoder_trunk(x_prompt, w_in, w_out, attn_norm, gqa_q_norm, gqa_k_norm, diff_lambda_q1,
                             diff_lambda_k1, diff_lambda_q2, diff_lambda_k2, diff_sub_norm, ffn_norm,
                             w_gate_up, w_down, final_norm)
    y_sample = encoder_trunk(x_sample, w_in, w_out, attn_norm, gqa_q_norm, gqa_k_norm, diff_lambda_q1,
                             diff_lambda_k1, diff_lambda_q2, diff_lambda_k2, diff_sub_norm, ffn_norm,
                             w_gate_up, w_down, final_norm)
    return (y_prompt, y_sample)
```

```python
import functools
import math

import jax
import jax.numpy as jnp
from jax import lax
from jax.experimental import pallas as pl
from jax.experimental.pallas import tpu as pltpu

D_MODEL = 1024
GRID_W = 64
HEAD_DIM = 64
AXIAL_DIM = HEAD_DIM // 2
GQA_HEADS = 8
GQA_KV_HEADS = 2
DIFF_HEADS = 4
DIFF_V_DIM = 2 * HEAD_DIM
GQA_Q = GQA_HEADS * HEAD_DIM
GQA_KV = GQA_KV_HEADS * HEAD_DIM
DIFF_QK = DIFF_HEADS * 2 * HEAD_DIM
DIFF_V = DIFF_HEADS * DIFF_V_DIM
IN_WIDTH = GQA_Q + 2 * GQA_KV + 2 * DIFF_QK + DIFF_V
D_FF = 2816
ROPE_THETA = 10000.0
NORM_EPS = 1e-6
DIFF_NORM_EPS = 1e-5

LANES = 128
VMEM_LIMIT_BYTES = 56 * 1024 * 1024
Q_PRESCALE = (HEAD_DIM ** -0.5) * math.log2(math.e)

OFF_QG = 0
OFF_KG = GQA_Q
OFF_VG = OFF_KG + GQA_KV
OFF_QD = OFF_VG + GQA_KV
OFF_KD = OFF_QD + DIFF_QK
OFF_VD = OFF_KD + DIFF_QK

BF16 = jnp.bfloat16
F32 = jnp.float32


def _dot(a, b):
    return jnp.dot(a, b, preferred_element_type=F32)


def _dot_nt(a, b):
    return lax.dot_general(a, b, (((1,), (1,)), ((), ())), preferred_element_type=F32)


def _rms(x, gain, eps):
    ms = jnp.mean(x * x, axis=-1, keepdims=True)
    return x * lax.rsqrt(ms + eps) * gain


def _proj_body(x_ref, an_ref, w_ref, qn_ref, kn_ref, ca_ref, sa_ref, cd_ref, sd_ref,
               qg_ref, kg_ref, vg_ref, qd_ref, kd_ref, vd_ref):
    tm = x_ref.shape[0]
    h = _rms(x_ref[...], an_ref[...], NORM_EPS).astype(BF16)
    proj = _dot(h, w_ref[...])

    lane = lax.broadcasted_iota(jnp.int32, (tm, LANES), 1)
    low_half = lane < HEAD_DIM
    r = lax.broadcasted_iota(jnp.int32, (LANES, LANES), 0) // HEAD_DIM
    c = lax.broadcasted_iota(jnp.int32, (LANES, LANES), 1) // HEAD_DIM
    blk = (r == c).astype(BF16)

    def head_norm(xc, gain):
        y = xc * xc
        hi = y.astype(BF16)
        lo = (y - hi.astype(F32)).astype(BF16)
        ss = _dot(hi, blk) + _dot(lo, blk)
        return xc * lax.rsqrt(ss * (1.0 / HEAD_DIM) + NORM_EPS) * gain

    def rope(xc, cos, sin_signed, half):
        fwd = pltpu.roll(xc, LANES - half, 1)
        bwd = pltpu.roll(xc, half, 1)
        partner = jnp.where((lane % (2 * half)) < half, fwd, bwd)
        return xc * cos + partner * sin_signed

    ca, sa, cd, sd = ca_ref[...], sa_ref[...], cd_ref[...], sd_ref[...]
    qn, kn = qn_ref[...], kn_ref[...]

    for j in range(GQA_Q // LANES):
        xc = proj[:, OFF_QG + j * LANES: OFF_QG + (j + 1) * LANES]
        xc = rope(head_norm(xc, qn), ca, sa, AXIAL_DIM // 2) * Q_PRESCALE
        qg_ref[:, j * LANES:(j + 1) * LANES] = xc.astype(BF16)

    kc = proj[:, OFF_KG:OFF_KG + LANES]
    kc = rope(head_norm(kc, kn), ca, sa, AXIAL_DIM // 2)
    kc_sw = pltpu.roll(kc, HEAD_DIM, 1)
    zero = jnp.zeros_like(kc)
    kg_ref[:, 0 * LANES:1 * LANES] = jnp.where(low_half, kc, zero).astype(BF16)
    kg_ref[:, 1 * LANES:2 * LANES] = jnp.where(low_half, zero, kc_sw).astype(BF16)
    kg_ref[:, 2 * LANES:3 * LANES] = jnp.where(low_half, kc_sw, zero).astype(BF16)
    kg_ref[:, 3 * LANES:4 * LANES] = jnp.where(low_half, zero, kc).astype(BF16)

    vc = proj[:, OFF_VG:OFF_VG + LANES]
    vc_sw = pltpu.roll(vc, HEAD_DIM, 1)
    one = jnp.ones_like(vc)
    vg_ref[:, 0 * LANES:1 * LANES] = jnp.where(low_half, vc, one).astype(BF16)
    vg_ref[:, 1 * LANES:2 * LANES] = jnp.where(low_half, vc_sw, one).astype(BF16)

    for j in range(DIFF_HEADS):
        xc = proj[:, OFF_QD + j * LANES: OFF_QD + (j + 1) * LANES]
        qd_ref[:, j * LANES:(j + 1) * LANES] = (rope(xc, cd, sd, HEAD_DIM // 2) * Q_PRESCALE).astype(BF16)
        kc = rope(proj[:, OFF_KD + j * LANES: OFF_KD + (j + 1) * LANES], cd, sd, HEAD_DIM // 2)
        zero = jnp.zeros_like(kc)
        kd_ref[:, (2 * j) * LANES:(2 * j + 1) * LANES] = jnp.where(low_half, kc, zero).astype(BF16)
        kd_ref[:, (2 * j + 1) * LANES:(2 * j + 2) * LANES] = jnp.where(low_half, zero, kc).astype(BF16)
        vc = proj[:, OFF_VD + j * LANES: OFF_VD + (j + 1) * LANES]
        vd_ref[:, (2 * j) * LANES:(2 * j + 1) * LANES] = vc.astype(BF16)
        vd_ref[:, (2 * j + 1) * LANES:(2 * j + 2) * LANES] = jnp.ones_like(vc).astype(BF16)


def _proj_call(x2d, seq, attn_norm, w_in, qn, kn, ca, sa, cd, sd, tm):
    tokens = x2d.shape[0]
    n_pos = seq // tm
    tok = lambda i: (i, 0)
    pos = lambda i: (i % n_pos, 0)
    const = lambda i: (0, 0)
    resident = functools.partial(pl.BlockSpec, index_map=const, pipeline_mode=pl.Buffered(1))
    widths = (GQA_Q, 4 * LANES, 2 * LANES, DIFF_QK, 2 * DIFF_QK, 2 * DIFF_V)
    return pl.pallas_call(
        _proj_body,
        grid=(tokens // tm,),
        in_specs=[
            pl.BlockSpec((tm, D_MODEL), tok),
            resident((1, D_MODEL)),
            resident((D_MODEL, IN_WIDTH)),
            resident((1, LANES)),
            resident((1, LANES)),
            pl.BlockSpec((tm, LANES), pos),
            pl.BlockSpec((tm, LANES), pos),
            pl.BlockSpec((tm, LANES), pos),
            pl.BlockSpec((tm, LANES), pos),
        ],
        out_specs=[pl.BlockSpec((tm, w), tok) for w in widths],
        out_shape=[jax.ShapeDtypeStruct((tokens, w), BF16) for w in widths],
        compiler_params=pltpu.CompilerParams(
            dimension_semantics=("parallel",), vmem_limit_bytes=VMEM_LIMIT_BYTES),
        name="proj",
    )(x2d, attn_norm, w_in, qn, kn, ca, sa, cd, sd)


def _softmax_pv(s, v):
    m = jnp.max(s, axis=-1, keepdims=True)
    p = jnp.exp2(s - m).astype(BF16)
    return _dot(p, v)


def _gqa_body(q_ref, k_ref, v_ref, o_ref):
    tq = q_ref.shape[0]
    q = q_ref[...]
    k = k_ref[...]
    v = v_ref[...]
    q2 = jnp.concatenate([q[:, :LANES], q[:, LANES:]], axis=0)
    r_lo = _softmax_pv(_dot_nt(q2, k[:, :LANES]), v)
    r_hi = _softmax_pv(_dot_nt(q2, k[:, LANES:]), v)
    low_half = lax.broadcasted_iota(jnp.int32, (tq, LANES), 1) < HEAD_DIM
    for col in range(2):
        lo = r_lo[col * tq:(col + 1) * tq]
        hi = r_hi[col * tq:(col + 1) * tq]
        lo_sw = pltpu.roll(lo, HEAD_DIM, 1)
        hi_sw = pltpu.roll(hi, HEAD_DIM, 1)
        out = jnp.where(low_half, lo / lo_sw, hi_sw / hi)
        o_ref[:, col * LANES:(col + 1) * LANES] = out.astype(o_ref.dtype)


def _gqa_call(qg, kg, vg, tq):
    batch, seq, _ = qg.shape
    return pl.pallas_call(
        _gqa_body,
        grid=(batch, GQA_KV_HEADS, seq // tq),
        in_specs=[
            pl.BlockSpec((None, tq, 2 * LANES), lambda b, g, i: (b, i, g)),
            pl.BlockSpec((None, seq, 2 * LANES), lambda b, g, i: (b, 0, g)),
            pl.BlockSpec((None, seq, LANES), lambda b, g, i: (b, 0, g)),
        ],
        out_specs=pl.BlockSpec((None, tq, 2 * LANES), lambda b, g, i: (b, i, g)),
        out_shape=jax.ShapeDtypeStruct((batch, seq, GQA_Q), BF16),
        compiler_params=pltpu.CompilerParams(
            dimension_semantics=("parallel", "parallel", "arbitrary"),
            vmem_limit_bytes=VMEM_LIMIT_BYTES),
        name="gqa_attn",
    )(qg, kg, vg)


def _diff_body(lambda_init, q_ref, k_ref, v_ref, lq1_ref, lk1_ref, lq2_ref, lk2_ref, sn_ref, o_ref):
    q = q_ref[...]
    k = k_ref[...]
    v = v_ref[...]
    lam = (jnp.exp(jnp.sum(lq1_ref[...] * lk1_ref[...], axis=-1, keepdims=True))
           - jnp.exp(jnp.sum(lq2_ref[...] * lk2_ref[...], axis=-1, keepdims=True)) + lambda_init)
    r1 = _softmax_pv(_dot_nt(q, k[:, :LANES]), v)
    r2 = _softmax_pv(_dot_nt(q, k[:, LANES:]), v)
    o = r1[:, :LANES] / r1[:, LANES:] - lam * (r2[:, :LANES] / r2[:, LANES:])
    o = _rms(o, sn_ref[...], DIFF_NORM_EPS) * (1.0 - lambda_init)
    o_ref[...] = o.astype(o_ref.dtype)


def _diff_call(qd, kd, vd, lq1, lk1, lq2, lk2, sub_norm, lambda_init, tq):
    batch, seq, _ = qd.shape
    const = lambda b, h, i: (0, 0)
    return pl.pallas_call(
        functools.partial(_diff_body, lambda_init),
        grid=(batch, DIFF_HEADS, seq // tq),
        in_specs=[
            pl.BlockSpec((None, tq, LANES), lambda b, h, i: (b, i, h)),
            pl.BlockSpec((None, seq, 2 * LANES), lambda b, h, i: (b, 0, h)),
            pl.BlockSpec((None, seq, 2 * LANES), lambda b, h, i: (b, 0, h)),
            pl.BlockSpec((1, HEAD_DIM), const),
            pl.BlockSpec((1, HEAD_DIM), const),
            pl.BlockSpec((1, HEAD_DIM), const),
            pl.BlockSpec((1, HEAD_DIM), const),
            pl.BlockSpec((1, DIFF_V_DIM), const),
        ],
        out_specs=pl.BlockSpec((None, tq, LANES), lambda b, h, i: (b, i, h)),
        out_shape=jax.ShapeDtypeStruct((batch, seq, DIFF_V), BF16),
        compiler_params=pltpu.CompilerParams(
            dimension_semantics=("parallel", "parallel", "arbitrary"),
            vmem_limit_bytes=VMEM_LIMIT_BYTES),
        name="diff_attn",
    )(qd, kd, vd, lq1, lk1, lq2, lk2, sub_norm)


FF_CHUNK = D_FF // 2


def _mlp_body(apply_final_norm, x_ref, og_ref, od_ref, wo_ref, fn_ref, wgu_ref, wd_ref, final_ref, y_ref):
    x = x_ref[...] + (_dot(og_ref[...], wo_ref[:GQA_Q, :]) + _dot(od_ref[...], wo_ref[GQA_Q:, :]))
    h = _rms(x, fn_ref[...], NORM_EPS).astype(BF16)
    ffn = None
    for c0 in range(0, D_FF, FF_CHUNK):
        gate = _dot(h, wgu_ref[:, c0:c0 + FF_CHUNK])
        up = _dot(h, wgu_ref[:, D_FF + c0:D_FF + c0 + FF_CHUNK])
        act = (gate * jax.nn.sigmoid(gate) * up).astype(BF16)
        down = _dot(act, wd_ref[c0:c0 + FF_CHUNK, :])
        ffn = down if ffn is None else ffn + down
    x = x + ffn
    if apply_final_norm:
        x = _rms(x, final_ref[...], NORM_EPS)
    y_ref[...] = x


def _mlp_call(x2d, og, od, w_out, ffn_norm, w_gate_up, w_down, final_norm, apply_final_norm, tm):
    tokens = x2d.shape[0]
    tok = lambda i: (i, 0)
    const = lambda i: (0, 0)
    resident = functools.partial(pl.BlockSpec, index_map=const, pipeline_mode=pl.Buffered(1))
    return pl.pallas_call(
        functools.partial(_mlp_body, apply_final_norm),
        grid=(tokens // tm,),
        in_specs=[
            pl.BlockSpec((tm, D_MODEL), tok),
            pl.BlockSpec((tm, GQA_Q), tok),
            pl.BlockSpec((tm, DIFF_V), tok),
            resident((GQA_Q + DIFF_V, D_MODEL)),
            resident((1, D_MODEL)),
            resident((D_MODEL, 2 * D_FF)),
            resident((D_FF, D_MODEL)),
            resident((1, D_MODEL)),
        ],
        out_specs=pl.BlockSpec((tm, D_MODEL), tok),
        out_shape=jax.ShapeDtypeStruct((tokens, D_MODEL), F32),
        compiler_params=pltpu.CompilerParams(
            dimension_semantics=("parallel",), vmem_limit_bytes=VMEM_LIMIT_BYTES),
        name="mlp",
    )(x2d, og, od, w_out, ffn_norm, w_gate_up, w_down, final_norm)


def _rope_tables(seq):
    def angles(pos, dim):
        inv = ROPE_THETA ** (-jnp.arange(0, dim, 2, dtype=F32) / dim)
        ang = pos.astype(F32)[:, None] * inv[None, :]
        return jnp.cos(ang), jnp.sin(ang)

    rows = seq // GRID_W
    row = jnp.repeat(jnp.arange(rows, dtype=jnp.int32), GRID_W)
    col = jnp.tile(jnp.arange(GRID_W, dtype=jnp.int32), rows)
    t = jnp.arange(seq, dtype=jnp.int32)
    rc, rs = angles(row, AXIAL_DIM)
    cc, cs = angles(col, AXIAL_DIM)
    tc, ts = angles(t, HEAD_DIM)
    cos_a = jnp.concatenate([rc, rc, cc, cc] * 2, axis=-1)
    sin_a = jnp.concatenate([-rs, rs, -cs, cs] * 2, axis=-1)
    cos_d = jnp.concatenate([tc, tc] * 2, axis=-1)
    sin_d = jnp.concatenate([-ts, ts] * 2, axis=-1)
    return cos_a, sin_a, cos_d, sin_d


def _trunk(x, params, tm, tq):
    batch, seq, _ = x.shape
    tokens = batch * seq
    (w_in, w_out, attn_norm, qn, kn, lq1, lk1, lq2, lk2, sub_norm, ffn_norm, w_gate_up, w_down,
     final_norm) = params
    depth = w_in.shape[0]
    tables = _rope_tables(seq)
    x2d = x.reshape(tokens, D_MODEL)
    for l in range(depth):
        lambda_init = 0.8 - 0.6 * math.exp(-0.3 * l)
        qg, kg, vg, qd, kd, vd = _proj_call(x2d, seq, attn_norm[l], w_in[l], qn[l], kn[l], *tables, tm)
        shape3 = lambda a: a.reshape(batch, seq, a.shape[-1])
        og = _gqa_call(shape3(qg), shape3(kg), shape3(vg), tq)
        od = _diff_call(shape3(qd), shape3(kd), shape3(vd), lq1[l], lk1[l], lq2[l], lk2[l],
                        sub_norm[l], lambda_init, tq)
        x2d = _mlp_call(x2d, og.reshape(tokens, GQA_Q), od.reshape(tokens, DIFF_V), w_out[l], ffn_norm[l],
                        w_gate_up[l], w_down[l], final_norm, l == depth - 1, tm)
    return x2d.reshape(batch, seq, D_MODEL)


def kernel(x_prompt, x_sample, w_in, w_out, attn_norm, gqa_q_norm, gqa_k_norm, diff_lambda_q1, diff_lambda_k1, diff_lambda_q2, diff_lambda_k2, diff_sub_norm, ffn_norm, w_gate_up, w_down, final_norm):
    depth = w_in.shape[0]
    row = lambda a: a.reshape(depth, 1, a.shape[-1])
    tile2 = lambda a: jnp.tile(a, (1, LANES // HEAD_DIM)).reshape(depth, 1, LANES)
    params = (
        w_in.astype(BF16), w_out.astype(BF16), row(attn_norm), tile2(gqa_q_norm), tile2(gqa_k_norm),
        row(diff_lambda_q1), row(diff_lambda_k1), row(diff_lambda_q2), row(diff_lambda_k2),
        row(diff_sub_norm), row(ffn_norm), w_gate_up.astype(BF16), w_down.astype(BF16),
        final_norm.reshape(1, D_MODEL),
    )
    y_prompt = _trunk(x_prompt, params, tm=512, tq=256)
    y_sample = _trunk(x_sample, params, tm=512, tq=256)
    return (y_prompt, y_sample)
```

```python
import functools
import math

import jax
import jax.numpy as jnp
from jax import lax
from jax.experimental import pallas as pl
from jax.experimental.pallas import tpu as pltpu

D_MODEL = 1024
GRID_W = 64
HEAD_DIM = 64
AXIAL_DIM = HEAD_DIM // 2
GQA_HEADS = 8
GQA_KV_HEADS = 2
DIFF_HEADS = 4
DIFF_V_DIM = 2 * HEAD_DIM
GQA_Q = GQA_HEADS * HEAD_DIM
GQA_KV = GQA_KV_HEADS * HEAD_DIM
DIFF_QK = DIFF_HEADS * 2 * HEAD_DIM
DIFF_V = DIFF_HEADS * DIFF_V_DIM
IN_WIDTH = GQA_Q + 2 * GQA_KV + 2 * DIFF_QK + DIFF_V
D_FF = 2816
ROPE_THETA = 10000.0
NORM_EPS = 1e-6
DIFF_NORM_EPS = 1e-5

LANES = 128
VMEM_LIMIT_BYTES = 56 * 1024 * 1024
Q_PRESCALE = (HEAD_DIM ** -0.5) * math.log2(math.e)

OFF_QG = 0
OFF_KG = GQA_Q
OFF_VG = OFF_KG + GQA_KV
OFF_QD = OFF_VG + GQA_KV
OFF_KD = OFF_QD + DIFF_QK
OFF_VD = OFF_KD + DIFF_QK

BF16 = jnp.bfloat16
F32 = jnp.float32


def _dot(a, b):
    return jnp.dot(a, b, preferred_element_type=F32)


def _dot_nt(a, b):
    return lax.dot_general(a, b, (((1,), (1,)), ((), ())), preferred_element_type=F32)


def _rms(x, gain, eps):
    ms = jnp.mean(x * x, axis=-1, keepdims=True)
    return x * lax.rsqrt(ms + eps) * gain


def _proj_body(x_ref, an_ref, w_ref, qn_ref, kn_ref, ca_ref, sa_ref, cd_ref, sd_ref,
               qg_ref, kg_ref, vg_ref, qd_ref, kd_ref, vd_ref):
    tm = x_ref.shape[0]
    h = _rms(x_ref[...], an_ref[...], NORM_EPS).astype(BF16)
    proj = _dot(h, w_ref[...])

    lane = lax.broadcasted_iota(jnp.int32, (tm, LANES), 1)
    low_half = lane < HEAD_DIM
    r = lax.broadcasted_iota(jnp.int32, (LANES, LANES), 0) // HEAD_DIM
    c = lax.broadcasted_iota(jnp.int32, (LANES, LANES), 1) // HEAD_DIM
    blk = (r == c).astype(BF16)

    def head_norm(xc, gain):
        y = xc * xc
        hi = y.astype(BF16)
        lo = (y - hi.astype(F32)).astype(BF16)
        ss = _dot(hi, blk) + _dot(lo, blk)
        return xc * lax.rsqrt(ss * (1.0 / HEAD_DIM) + NORM_EPS) * gain

    def rope(xc, cos, sin_signed, half):
        fwd = pltpu.roll(xc, LANES - half, 1)
        bwd = pltpu.roll(xc, half, 1)
        partner = jnp.where((lane % (2 * half)) < half, fwd, bwd)
        return xc * cos + partner * sin_signed

    ca, sa, cd, sd = ca_ref[...], sa_ref[...], cd_ref[...], sd_ref[...]
    qn, kn = qn_ref[...], kn_ref[...]

    for j in range(GQA_Q // LANES):
        xc = proj[:, OFF_QG + j * LANES: OFF_QG + (j + 1) * LANES]
        xc = rope(head_norm(xc, qn), ca, sa, AXIAL_DIM // 2) * Q_PRESCALE
        qg_ref[:, j * LANES:(j + 1) * LANES] = xc.astype(BF16)

    kc = proj[:, OFF_KG:OFF_KG + LANES]
    kc = rope(head_norm(kc, kn), ca, sa, AXIAL_DIM // 2)
    kc_sw = pltpu.roll(kc, HEAD_DIM, 1)
    zero = jnp.zeros_like(kc)
    kg_ref[:, 0 * LANES:1 * LANES] = jnp.where(low_half, kc, zero).astype(BF16)
    kg_ref[:, 1 * LANES:2 * LANES] = jnp.where(low_half, zero, kc_sw).astype(BF16)
    kg_ref[:, 2 * LANES:3 * LANES] = jnp.where(low_half, kc_sw, zero).astype(BF16)
    kg_ref[:, 3 * LANES:4 * LANES] = jnp.where(low_half, zero, kc).astype(BF16)

    vc = proj[:, OFF_VG:OFF_VG + LANES]
    vc_sw = pltpu.roll(vc, HEAD_DIM, 1)
    one = jnp.ones_like(vc)
    vg_ref[:, 0 * LANES:1 * LANES] = jnp.where(low_half, vc, one).astype(BF16)
    vg_ref[:, 1 * LANES:2 * LANES] = jnp.where(low_half, vc_sw, one).astype(BF16)

    for j in range(DIFF_HEADS):
        xc = proj[:, OFF_QD + j * LANES: OFF_QD + (j + 1) * LANES]
        qd_ref[:, j * LANES:(j + 1) * LANES] = (rope(xc, cd, sd, HEAD_DIM // 2) * Q_PRESCALE).astype(BF16)
        kc = rope(proj[:, OFF_KD + j * LANES: OFF_KD + (j + 1) * LANES], cd, sd, HEAD_DIM // 2)
        zero = jnp.zeros_like(kc)
        kd_ref[:, (2 * j) * LANES:(2 * j + 1) * LANES] = jnp.where(low_half, kc, zero).astype(BF16)
        kd_ref[:, (2 * j + 1) * LANES:(2 * j + 2) * LANES] = jnp.where(low_half, zero, kc).astype(BF16)
        vc = proj[:, OFF_VD + j * LANES: OFF_VD + (j + 1) * LANES]
        vd_ref[:, (2 * j) * LANES:(2 * j + 1) * LANES] = vc.astype(BF16)
        vd_ref[:, (2 * j + 1) * LANES:(2 * j + 2) * LANES] = jnp.ones_like(vc).astype(BF16)


def _proj_call(x2d, seq, attn_norm, w_in, qn, kn, ca, sa, cd, sd, tm):
    tokens = x2d.shape[0]
    n_pos = seq // tm
    tok = lambda i: (i, 0)
    pos = lambda i: (i % n_pos, 0)
    const = lambda i: (0, 0)
    resident = functools.partial(pl.BlockSpec, index_map=const, pipeline_mode=pl.Buffered(1))
    widths = (GQA_Q, 4 * LANES, 2 * LANES, DIFF_QK, 2 * DIFF_QK, 2 * DIFF_V)
    return pl.pallas_call(
        _proj_body,
        grid=(tokens // tm,),
        in_specs=[
            pl.BlockSpec((tm, D_MODEL), tok),
            resident((1, D_MODEL)),
            resident((D_MODEL, IN_WIDTH)),
            resident((1, LANES)),
            resident((1, LANES)),
            pl.BlockSpec((tm, LANES), pos),
            pl.BlockSpec((tm, LANES), pos),
            pl.BlockSpec((tm, LANES), pos),
            pl.BlockSpec((tm, LANES), pos),
        ],
        out_specs=[pl.BlockSpec((tm, w), tok) for w in widths],
        out_shape=[jax.ShapeDtypeStruct((tokens, w), BF16) for w in widths],
        compiler_params=pltpu.CompilerParams(
            dimension_semantics=("parallel",), vmem_limit_bytes=VMEM_LIMIT_BYTES),
        name="proj",
    )(x2d, attn_norm, w_in, qn, kn, ca, sa, cd, sd)


def _softmax_pv(s, v):
    m = jnp.max(s, axis=-1, keepdims=True)
    p = jnp.exp2(s - m).astype(BF16)
    return _dot(p, v)


def _pipelined_rows(n_blocks, scores, consume):
    assert n_blocks % 2 == 0 and n_blocks >= 2
    scores(0, 0)

    def pair(j, carry):
        scores(2 * j + 1, 1)
        consume(2 * j, 0)
        scores(2 * j + 2, 0)
        consume(2 * j + 1, 1)
        return carry

    lax.fori_loop(0, n_blocks // 2 - 1, pair, 0)
    scores(n_blocks - 1, 1)
    consume(n_blocks - 2, 0)
    consume(n_blocks - 1, 1)


def _gqa_body(rows, q_ref, k_ref, v_ref, o_ref, s_scr0, s_scr1):
    s_scr = (s_scr0, s_scr1)
    seq = q_ref.shape[0]
    low_half = lax.broadcasted_iota(jnp.int32, (rows, LANES), 1) < HEAD_DIM

    def scores(i, slot):
        q = q_ref[pl.ds(pl.multiple_of(i * rows, rows), rows), :]
        q2 = jnp.concatenate([q[:, :LANES], q[:, LANES:]], axis=0)
        s_scr[slot][0] = _dot_nt(q2, k_ref[:, :LANES])
        s_scr[slot][1] = _dot_nt(q2, k_ref[:, LANES:])

    def consume(i, slot):
        v = v_ref[...]
        r_lo = _softmax_pv(s_scr[slot][0], v)
        r_hi = _softmax_pv(s_scr[slot][1], v)
        cols = []
        for col in range(2):
            lo = r_lo[col * rows:(col + 1) * rows]
            hi = r_hi[col * rows:(col + 1) * rows]
            lo_sw = pltpu.roll(lo, HEAD_DIM, 1)
            hi_sw = pltpu.roll(hi, HEAD_DIM, 1)
            cols.append(jnp.where(low_half, lo / lo_sw, hi_sw / hi))
        out = jnp.concatenate(cols, axis=1)
        o_ref[pl.ds(pl.multiple_of(i * rows, rows), rows), :] = out.astype(o_ref.dtype)

    _pipelined_rows(seq // rows, scores, consume)


def _gqa_call(qg, kg, vg, rows):
    batch, seq, _ = qg.shape
    return pl.pallas_call(
        functools.partial(_gqa_body, rows),
        grid=(batch, GQA_KV_HEADS),
        in_specs=[
            pl.BlockSpec((None, seq, 2 * LANES), lambda b, g: (b, 0, g)),
            pl.BlockSpec((None, seq, 2 * LANES), lambda b, g: (b, 0, g)),
            pl.BlockSpec((None, seq, LANES), lambda b, g: (b, 0, g)),
        ],
        out_specs=pl.BlockSpec((None, seq, 2 * LANES), lambda b, g: (b, 0, g)),
        out_shape=jax.ShapeDtypeStruct((batch, seq, GQA_Q), BF16),
        scratch_shapes=[pltpu.VMEM((2, 2 * rows, seq), F32)] * 2,
        compiler_params=pltpu.CompilerParams(
            dimension_semantics=("parallel", "parallel"),
            vmem_limit_bytes=VMEM_LIMIT_BYTES),
        name="gqa_attn",
    )(qg, kg, vg)


def _diff_body(lambda_init, rows, q_ref, k_ref, v_ref, lq1_ref, lk1_ref, lq2_ref, lk2_ref, sn_ref, o_ref,
               s_scr0, s_scr1):
    s_scr = (s_scr0, s_scr1)
    seq = q_ref.shape[0]
    lam = (jnp.exp(jnp.sum(lq1_ref[...] * lk1_ref[...], axis=-1, keepdims=True))
           - jnp.exp(jnp.sum(lq2_ref[...] * lk2_ref[...], axis=-1, keepdims=True)) + lambda_init)

    def scores(i, slot):
        q = q_ref[pl.ds(pl.multiple_of(i * rows, rows), rows), :]
        s_scr[slot][0] = _dot_nt(q, k_ref[:, :LANES])
        s_scr[slot][1] = _dot_nt(q, k_ref[:, LANES:])

    def consume(i, slot):
        v = v_ref[...]
        r1 = _softmax_pv(s_scr[slot][0], v)
        r2 = _softmax_pv(s_scr[slot][1], v)
        o = r1[:, :LANES] / r1[:, LANES:] - lam * (r2[:, :LANES] / r2[:, LANES:])
        o = _rms(o, sn_ref[...], DIFF_NORM_EPS) * (1.0 - lambda_init)
        o_ref[pl.ds(pl.multiple_of(i * rows, rows), rows), :] = o.astype(o_ref.dtype)

    _pipelined_rows(seq // rows, scores, consume)


def _diff_call(qd, kd, vd, lq1, lk1, lq2, lk2, sub_norm, lambda_init, rows):
    batch, seq, _ = qd.shape
    const = lambda b, h: (0, 0)
    return pl.pallas_call(
        functools.partial(_diff_body, lambda_init, rows),
        grid=(batch, DIFF_HEADS),
        in_specs=[
            pl.BlockSpec((None, seq, LANES), lambda b, h: (b, 0, h)),
            pl.BlockSpec((None, seq, 2 * LANES), lambda b, h: (b, 0, h)),
            pl.BlockSpec((None, seq, 2 * LANES), lambda b, h: (b, 0, h)),
            pl.BlockSpec((1, HEAD_DIM), const),
            pl.BlockSpec((1, HEAD_DIM), const),
            pl.BlockSpec((1, HEAD_DIM), const),
            pl.BlockSpec((1, HEAD_DIM), const),
            pl.BlockSpec((1, DIFF_V_DIM), const),
        ],
        out_specs=pl.BlockSpec((None, seq, LANES), lambda b, h: (b, 0, h)),
        out_shape=jax.ShapeDtypeStruct((batch, seq, DIFF_V), BF16),
        scratch_shapes=[pltpu.VMEM((2, rows, seq), F32)] * 2,
        compiler_params=pltpu.CompilerParams(
            dimension_semantics=("parallel", "parallel"),
            vmem_limit_bytes=VMEM_LIMIT_BYTES),
        name="diff_attn",
    )(qd, kd, vd, lq1, lk1, lq2, lk2, sub_norm)


FF_CHUNK = D_FF // 2


def _mlp_body(apply_final_norm, x_ref, og_ref, od_ref, wo_ref, fn_ref, wgu_ref, wd_ref, final_ref, y_ref):
    x = x_ref[...] + (_dot(og_ref[...], wo_ref[:GQA_Q, :]) + _dot(od_ref[...], wo_ref[GQA_Q:, :]))
    h = _rms(x, fn_ref[...], NORM_EPS).astype(BF16)
    ffn = None
    for c0 in range(0, D_FF, FF_CHUNK):
        gate = _dot(h, wgu_ref[:, c0:c0 + FF_CHUNK])
        up = _dot(h, wgu_ref[:, D_FF + c0:D_FF + c0 + FF_CHUNK])
        act = (gate * jax.nn.sigmoid(gate) * up).astype(BF16)
        down = _dot(act, wd_ref[c0:c0 + FF_CHUNK, :])
        ffn = down if ffn is None else ffn + down
    x = x + ffn
    if apply_final_norm:
        x = _rms(x, final_ref[...], NORM_EPS)
    y_ref[...] = x


def _mlp_call(x2d, og, od, w_out, ffn_norm, w_gate_up, w_down, final_norm, apply_final_norm, tm):
    tokens = x2d.shape[0]
    tok = lambda i: (i, 0)
    const = lambda i: (0, 0)
    resident = functools.partial(pl.BlockSpec, index_map=const, pipeline_mode=pl.Buffered(1))
    return pl.pallas_call(
        functools.partial(_mlp_body, apply_final_norm),
        grid=(tokens // tm,),
        in_specs=[
            pl.BlockSpec((tm, D_MODEL), tok),
            pl.BlockSpec((tm, GQA_Q), tok),
            pl.BlockSpec((tm, DIFF_V), tok),
            resident((GQA_Q + DIFF_V, D_MODEL)),
            resident((1, D_MODEL)),
            resident((D_MODEL, 2 * D_FF)),
            resident((D_FF, D_MODEL)),
            resident((1, D_MODEL)),
        ],
        out_specs=pl.BlockSpec((tm, D_MODEL), tok),
        out_shape=jax.ShapeDtypeStruct((tokens, D_MODEL), F32),
        compiler_params=pltpu.CompilerParams(
            dimension_semantics=("parallel",), vmem_limit_bytes=VMEM_LIMIT_BYTES),
        name="mlp",
    )(x2d, og, od, w_out, ffn_norm, w_gate_up, w_down, final_norm)


def _rope_tables(seq):
    def angles(pos, dim):
        inv = ROPE_THETA ** (-jnp.arange(0, dim, 2, dtype=F32) / dim)
        ang = pos.astype(F32)[:, None] * inv[None, :]
        return jnp.cos(ang), jnp.sin(ang)

    rows = seq // GRID_W
    row = jnp.repeat(jnp.arange(rows, dtype=jnp.int32), GRID_W)
    col = jnp.tile(jnp.arange(GRID_W, dtype=jnp.int32), rows)
    t = jnp.arange(seq, dtype=jnp.int32)
    rc, rs = angles(row, AXIAL_DIM)
    cc, cs = angles(col, AXIAL_DIM)
    tc, ts = angles(t, HEAD_DIM)
    cos_a = jnp.concatenate([rc, rc, cc, cc] * 2, axis=-1)
    sin_a = jnp.concatenate([-rs, rs, -cs, cs] * 2, axis=-1)
    cos_d = jnp.concatenate([tc, tc] * 2, axis=-1)
    sin_d = jnp.concatenate([-ts, ts] * 2, axis=-1)
    return cos_a, sin_a, cos_d, sin_d


def _trunk(x, params, tm, gqa_rows, diff_rows):
    batch, seq, _ = x.shape
    tokens = batch * seq
    (w_in, w_out, attn_norm, qn, kn, lq1, lk1, lq2, lk2, sub_norm, ffn_norm, w_gate_up, w_down,
     final_norm) = params
    depth = w_in.shape[0]
    tables = _rope_tables(seq)
    x2d = x.reshape(tokens, D_MODEL)
    for l in range(depth):
        lambda_init = 0.8 - 0.6 * math.exp(-0.3 * l)
        qg, kg, vg, qd, kd, vd = _proj_call(x2d, seq, attn_norm[l], w_in[l], qn[l], kn[l], *tables, tm)
        shape3 = lambda a: a.reshape(batch, seq, a.shape[-1])
        og = _gqa_call(shape3(qg), shape3(kg), shape3(vg), gqa_rows)
        od = _diff_call(shape3(qd), shape3(kd), shape3(vd), lq1[l], lk1[l], lq2[l], lk2[l],
                        sub_norm[l], lambda_init, diff_rows)
        x2d = _mlp_call(x2d, og.reshape(tokens, GQA_Q), od.reshape(tokens, DIFF_V), w_out[l], ffn_norm[l],
                        w_gate_up[l], w_down[l], final_norm, l == depth - 1, tm)
    return x2d.reshape(batch, seq, D_MODEL)


def kernel(x_prompt, x_sample, w_in, w_out, attn_norm, gqa_q_norm, gqa_k_norm, diff_lambda_q1, diff_lambda_k1, diff_lambda_q2, diff_lambda_k2, diff_sub_norm, ffn_norm, w_gate_up, w_down, final_norm):
    depth = w_in.shape[0]
    row = lambda a: a.reshape(depth, 1, a.shape[-1])
    tile2 = lambda a: jnp.tile(a, (1, LANES // HEAD_DIM)).reshape(depth, 1, LANES)
    params = (
        w_in.astype(BF16), w_out.astype(BF16), row(attn_norm), tile2(gqa_q_norm), tile2(gqa_k_norm),
        row(diff_lambda_q1), row(diff_lambda_k1), row(diff_lambda_q2), row(diff_lambda_k2),
        row(diff_sub_norm), row(ffn_norm), w_gate_up.astype(BF16), w_down.astype(BF16),
        final_norm.reshape(1, D_MODEL),
    )
    y_prompt = _trunk(x_prompt, params, tm=512, gqa_rows=128, diff_rows=256)
    y_sample = _trunk(x_sample, params, tm=512, gqa_rows=128, diff_rows=256)
    return (y_prompt, y_sample)
```

```python
import functools
import math

import jax
import jax.numpy as jnp
from jax import lax
from jax.experimental import pallas as pl
from jax.experimental.pallas import tpu as pltpu

D_MODEL = 1024
GRID_W = 64
HEAD_DIM = 64
AXIAL_DIM = HEAD_DIM // 2
GQA_HEADS = 8
GQA_KV_HEADS = 2
DIFF_HEADS = 4
DIFF_V_DIM = 2 * HEAD_DIM
GQA_Q = GQA_HEADS * HEAD_DIM
GQA_KV = GQA_KV_HEADS * HEAD_DIM
DIFF_QK = DIFF_HEADS * 2 * HEAD_DIM
DIFF_V = DIFF_HEADS * DIFF_V_DIM
IN_WIDTH = GQA_Q + 2 * GQA_KV + 2 * DIFF_QK + DIFF_V
D_FF = 2816
ROPE_THETA = 10000.0
NORM_EPS = 1e-6
DIFF_NORM_EPS = 1e-5

LANES = 128
VMEM_LIMIT_BYTES = 56 * 1024 * 1024
Q_PRESCALE = (HEAD_DIM ** -0.5) * math.log2(math.e)

OFF_QG = 0
OFF_KG = GQA_Q
OFF_VG = OFF_KG + GQA_KV
OFF_QD = OFF_VG + GQA_KV
OFF_KD = OFF_QD + DIFF_QK
OFF_VD = OFF_KD + DIFF_QK

BF16 = jnp.bfloat16
F32 = jnp.float32


def _dot(a, b):
    return jnp.dot(a, b, preferred_element_type=F32)


def _dot_nt(a, b):
    return lax.dot_general(a, b, (((1,), (1,)), ((), ())), preferred_element_type=F32)


def _rms(x, gain, eps):
    ms = jnp.mean(x * x, axis=-1, keepdims=True)
    return x * lax.rsqrt(ms + eps) * gain


def _proj_body(x_ref, an_ref, w_ref, qn_ref, kn_ref, ca_ref, sa_ref, cd_ref, sd_ref,
               qg_ref, kg_ref, vg_ref, qd_ref, kd_ref, vd_ref):
    tm = x_ref.shape[0]
    h = _rms(x_ref[...], an_ref[...], NORM_EPS).astype(BF16)
    proj = _dot(h, w_ref[...])

    lane = lax.broadcasted_iota(jnp.int32, (tm, LANES), 1)
    low_half = lane < HEAD_DIM
    r = (lax.broadcasted_iota(jnp.int32, (2 * LANES, LANES), 0) % LANES) // HEAD_DIM
    c = lax.broadcasted_iota(jnp.int32, (2 * LANES, LANES), 1) // HEAD_DIM
    blk = (r == c).astype(BF16)

    def head_norm(xc, gain):
        y = xc * xc
        hi = y.astype(BF16)
        lo = (y - hi.astype(F32)).astype(BF16)
        ss = _dot(jnp.concatenate([hi, lo], axis=1), blk)
        return xc * lax.rsqrt(ss * (1.0 / HEAD_DIM) + NORM_EPS) * gain

    def rope(xc, cos, sin_signed, half):
        fwd = pltpu.roll(xc, LANES - half, 1)
        bwd = pltpu.roll(xc, half, 1)
        partner = jnp.where((lane % (2 * half)) < half, fwd, bwd)
        return xc * cos + partner * sin_signed

    ca, sa, cd, sd = ca_ref[...], sa_ref[...], cd_ref[...], sd_ref[...]
    qn, kn = qn_ref[...], kn_ref[...]

    for j in range(GQA_Q // LANES):
        xc = proj[:, OFF_QG + j * LANES: OFF_QG + (j + 1) * LANES]
        xc = rope(head_norm(xc, qn), ca, sa, AXIAL_DIM // 2) * Q_PRESCALE
        qg_ref[:, j * LANES:(j + 1) * LANES] = xc.astype(BF16)

    kc = proj[:, OFF_KG:OFF_KG + LANES]
    kc = rope(head_norm(kc, kn), ca, sa, AXIAL_DIM // 2)
    kc_sw = pltpu.roll(kc, HEAD_DIM, 1)
    zero = jnp.zeros_like(kc)
    kg_ref[:, 0 * LANES:1 * LANES] = jnp.where(low_half, kc, zero).astype(BF16)
    kg_ref[:, 1 * LANES:2 * LANES] = jnp.where(low_half, zero, kc_sw).astype(BF16)
    kg_ref[:, 2 * LANES:3 * LANES] = jnp.where(low_half, kc_sw, zero).astype(BF16)
    kg_ref[:, 3 * LANES:4 * LANES] = jnp.where(low_half, zero, kc).astype(BF16)

    vc = proj[:, OFF_VG:OFF_VG + LANES]
    vc_sw = pltpu.roll(vc, HEAD_DIM, 1)
    one = jnp.ones_like(vc)
    vg_ref[:, 0 * LANES:1 * LANES] = jnp.where(low_half, vc, one).astype(BF16)
    vg_ref[:, 1 * LANES:2 * LANES] = jnp.where(low_half, vc_sw, one).astype(BF16)

    for j in range(DIFF_HEADS):
        xc = proj[:, OFF_QD + j * LANES: OFF_QD + (j + 1) * LANES]
        qd_ref[:, j * LANES:(j + 1) * LANES] = (rope(xc, cd, sd, HEAD_DIM // 2) * Q_PRESCALE).astype(BF16)
        kc = rope(proj[:, OFF_KD + j * LANES: OFF_KD + (j + 1) * LANES], cd, sd, HEAD_DIM // 2)
        zero = jnp.zeros_like(kc)
        kd_ref[:, (2 * j) * LANES:(2 * j + 1) * LANES] = jnp.where(low_half, kc, zero).astype(BF16)
        kd_ref[:, (2 * j + 1) * LANES:(2 * j + 2) * LANES] = jnp.where(low_half, zero, kc).astype(BF16)
        vc = proj[:, OFF_VD + j * LANES: OFF_VD + (j + 1) * LANES]
        vd_ref[:, (2 * j) * LANES:(2 * j + 1) * LANES] = vc.astype(BF16)
        vd_ref[:, (2 * j + 1) * LANES:(2 * j + 2) * LANES] = jnp.ones_like(vc).astype(BF16)


def _proj_call(x2d, seq, attn_norm, w_in, qn, kn, ca, sa, cd, sd, tm):
    tokens = x2d.shape[0]
    n_pos = seq // tm
    tok = lambda i: (i, 0)
    pos = lambda i: (i % n_pos, 0)
    const = lambda i: (0, 0)
    resident = functools.partial(pl.BlockSpec, index_map=const, pipeline_mode=pl.Buffered(1))
    widths = (GQA_Q, 4 * LANES, 2 * LANES, DIFF_QK, 2 * DIFF_QK, 2 * DIFF_V)
    return pl.pallas_call(
        _proj_body,
        grid=(tokens // tm,),
        in_specs=[
            pl.BlockSpec((tm, D_MODEL), tok),
            resident((1, D_MODEL)),
            resident((D_MODEL, IN_WIDTH)),
            resident((1, LANES)),
            resident((1, LANES)),
            pl.BlockSpec((tm, LANES), pos),
            pl.BlockSpec((tm, LANES), pos),
            pl.BlockSpec((tm, LANES), pos),
            pl.BlockSpec((tm, LANES), pos),
        ],
        out_specs=[pl.BlockSpec((tm, w), tok) for w in widths],
        out_shape=[jax.ShapeDtypeStruct((tokens, w), BF16) for w in widths],
        compiler_params=pltpu.CompilerParams(
            dimension_semantics=("parallel",), vmem_limit_bytes=VMEM_LIMIT_BYTES),
        name="proj",
    )(x2d, attn_norm, w_in, qn, kn, ca, sa, cd, sd)


def _softmax_pv(s, v):
    m = jnp.max(s, axis=-1, keepdims=True)
    p = jnp.exp2(s - m).astype(BF16)
    return _dot(p, v)


PIPELINE_BLOCKS_PER_TRIP = 2


def _pipelined_rows(n_blocks, scores, consume):
    per_trip = PIPELINE_BLOCKS_PER_TRIP
    assert per_trip % 2 == 0 and n_blocks >= 2
    scores(0, 0)
    looped = ((n_blocks - 1) // per_trip) * per_trip

    def trip(t, carry):
        for u in range(per_trip):
            i = t * per_trip + u
            scores(i + 1, (u + 1) % 2)
            consume(i, u % 2)
        return carry

    lax.fori_loop(0, looped // per_trip, trip, 0)
    for i in range(looped, n_blocks):
        if i + 1 < n_blocks:
            scores(i + 1, (i + 1) % 2)
        consume(i, i % 2)


def _gqa_body(rows, q_ref, k_ref, v_ref, o_ref, s_scr0, s_scr1):
    s_scr = (s_scr0, s_scr1)
    seq = q_ref.shape[0]
    low_half = lax.broadcasted_iota(jnp.int32, (rows, LANES), 1) < HEAD_DIM

    def scores(i, slot):
        q = q_ref[pl.ds(pl.multiple_of(i * rows, rows), rows), :]
        q2 = jnp.concatenate([q[:, :LANES], q[:, LANES:]], axis=0)
        s_scr[slot][0] = _dot_nt(q2, k_ref[:, :LANES])
        s_scr[slot][1] = _dot_nt(q2, k_ref[:, LANES:])

    def consume(i, slot):
        v = v_ref[...]
        r_lo = _softmax_pv(s_scr[slot][0], v)
        r_hi = _softmax_pv(s_scr[slot][1], v)
        cols = []
        for col in range(2):
            lo = r_lo[col * rows:(col + 1) * rows]
            hi = r_hi[col * rows:(col + 1) * rows]
            lo_sw = pltpu.roll(lo, HEAD_DIM, 1)
            hi_sw = pltpu.roll(hi, HEAD_DIM, 1)
            cols.append(jnp.where(low_half, lo / lo_sw, hi_sw / hi))
        out = jnp.concatenate(cols, axis=1)
        o_ref[pl.ds(pl.multiple_of(i * rows, rows), rows), :] = out.astype(o_ref.dtype)

    _pipelined_rows(seq // rows, scores, consume)


def _gqa_call(qg, kg, vg, rows):
    batch, seq, _ = qg.shape
    return pl.pallas_call(
        functools.partial(_gqa_body, rows),
        grid=(batch, GQA_KV_HEADS),
        in_specs=[
            pl.BlockSpec((None, seq, 2 * LANES), lambda b, g: (b, 0, g)),
            pl.BlockSpec((None, seq, 2 * LANES), lambda b, g: (b, 0, g)),
            pl.BlockSpec((None, seq, LANES), lambda b, g: (b, 0, g)),
        ],
        out_specs=pl.BlockSpec((None, seq, 2 * LANES), lambda b, g: (b, 0, g)),
        out_shape=jax.ShapeDtypeStruct((batch, seq, GQA_Q), BF16),
        scratch_shapes=[pltpu.VMEM((2, 2 * rows, seq), F32)] * 2,
        compiler_params=pltpu.CompilerParams(
            dimension_semantics=("parallel", "parallel"),
            vmem_limit_bytes=VMEM_LIMIT_BYTES),
        name="gqa_attn",
    )(qg, kg, vg)


def _diff_body(lambda_init, rows, q_ref, k_ref, v_ref, lq1_ref, lk1_ref, lq2_ref, lk2_ref, sn_ref, o_ref,
               s_scr0, s_scr1):
    s_scr = (s_scr0, s_scr1)
    seq = q_ref.shape[0]
    lam = (jnp.exp(jnp.sum(lq1_ref[...] * lk1_ref[...], axis=-1, keepdims=True))
           - jnp.exp(jnp.sum(lq2_ref[...] * lk2_ref[...], axis=-1, keepdims=True)) + lambda_init)

    def scores(i, slot):
        q = q_ref[pl.ds(pl.multiple_of(i * rows, rows), rows), :]
        s_scr[slot][0] = _dot_nt(q, k_ref[:, :LANES])
        s_scr[slot][1] = _dot_nt(q, k_ref[:, LANES:])

    def consume(i, slot):
        v = v_ref[...]
        r1 = _softmax_pv(s_scr[slot][0], v)
        r2 = _softmax_pv(s_scr[slot][1], v)
        o = r1[:, :LANES] / r1[:, LANES:] - lam * (r2[:, :LANES] / r2[:, LANES:])
        o = _rms(o, sn_ref[...], DIFF_NORM_EPS) * (1.0 - lambda_init)
        o_ref[pl.ds(pl.multiple_of(i * rows, rows), rows), :] = o.astype(o_ref.dtype)

    _pipelined_rows(seq // rows, scores, consume)


def _diff_call(qd, kd, vd, lq1, lk1, lq2, lk2, sub_norm, lambda_init, rows):
    batch, seq, _ = qd.shape
    const = lambda b, h: (0, 0)
    return pl.pallas_call(
        functools.partial(_diff_body, lambda_init, rows),
        grid=(batch, DIFF_HEADS),
        in_specs=[
            pl.BlockSpec((None, seq, LANES), lambda b, h: (b, 0, h)),
            pl.BlockSpec((None, seq, 2 * LANES), lambda b, h: (b, 0, h)),
            pl.BlockSpec((None, seq, 2 * LANES), lambda b, h: (b, 0, h)),
            pl.BlockSpec((1, HEAD_DIM), const),
            pl.BlockSpec((1, HEAD_DIM), const),
            pl.BlockSpec((1, HEAD_DIM), const),
            pl.BlockSpec((1, HEAD_DIM), const),
            pl.BlockSpec((1, DIFF_V_DIM), const),
        ],
        out_specs=pl.BlockSpec((None, seq, LANES), lambda b, h: (b, 0, h)),
        out_shape=jax.ShapeDtypeStruct((batch, seq, DIFF_V), BF16),
        scratch_shapes=[pltpu.VMEM((2, rows, seq), F32)] * 2,
        compiler_params=pltpu.CompilerParams(
            dimension_semantics=("parallel", "parallel"),
            vmem_limit_bytes=VMEM_LIMIT_BYTES),
        name="diff_attn",
    )(qd, kd, vd, lq1, lk1, lq2, lk2, sub_norm)


MXU_TILE = 256
FF_SPLITS = (0, 6 * MXU_TILE, D_FF)


def _mlp_body(apply_final_norm, x_ref, og_ref, od_ref, wo_ref, fn_ref, wgu_ref, wd_ref, final_ref, y_ref):
    x = x_ref[...] + (_dot(og_ref[...], wo_ref[:GQA_Q, :]) + _dot(od_ref[...], wo_ref[GQA_Q:, :]))
    h = _rms(x, fn_ref[...], NORM_EPS).astype(BF16)
    ffn = None
    for c0, c1 in zip(FF_SPLITS[:-1], FF_SPLITS[1:]):
        gate = _dot(h, wgu_ref[:, c0:c1])
        up = _dot(h, wgu_ref[:, D_FF + c0:D_FF + c1])
        act = (gate * jax.nn.sigmoid(gate) * up).astype(BF16)
        down = _dot(act, wd_ref[c0:c1, :])
        ffn = down if ffn is None else ffn + down
    x = x + ffn
    if apply_final_norm:
        x = _rms(x, final_ref[...], NORM_EPS)
    y_ref[...] = x


def _mlp_call(x2d, og, od, w_out, ffn_norm, w_gate_up, w_down, final_norm, apply_final_norm, tm):
    tokens = x2d.shape[0]
    tok = lambda i: (i, 0)
    const = lambda i: (0, 0)
    resident = functools.partial(pl.BlockSpec, index_map=const, pipeline_mode=pl.Buffered(1))
    return pl.pallas_call(
        functools.partial(_mlp_body, apply_final_norm),
        grid=(tokens // tm,),
        in_specs=[
            pl.BlockSpec((tm, D_MODEL), tok),
            pl.BlockSpec((tm, GQA_Q), tok),
            pl.BlockSpec((tm, DIFF_V), tok),
            resident((GQA_Q + DIFF_V, D_MODEL)),
            resident((1, D_MODEL)),
            resident((D_MODEL, 2 * D_FF)),
            resident((D_FF, D_MODEL)),
            resident((1, D_MODEL)),
        ],
        out_specs=pl.BlockSpec((tm, D_MODEL), tok),
        out_shape=jax.ShapeDtypeStruct((tokens, D_MODEL), F32),
        compiler_params=pltpu.CompilerParams(
            dimension_semantics=("parallel",), vmem_limit_bytes=VMEM_LIMIT_BYTES),
        name="mlp",
    )(x2d, og, od, w_out, ffn_norm, w_gate_up, w_down, final_norm)


def _rope_tables(seq):
    def angles(pos, dim):
        inv = ROPE_THETA ** (-jnp.arange(0, dim, 2, dtype=F32) / dim)
        ang = pos.astype(F32)[:, None] * inv[None, :]
        return jnp.cos(ang), jnp.sin(ang)

    rows = seq // GRID_W
    row = jnp.repeat(jnp.arange(rows, dtype=jnp.int32), GRID_W)
    col = jnp.tile(jnp.arange(GRID_W, dtype=jnp.int32), rows)
    t = jnp.arange(seq, dtype=jnp.int32)
    rc, rs = angles(row, AXIAL_DIM)
    cc, cs = angles(col, AXIAL_DIM)
    tc, ts = angles(t, HEAD_DIM)
    cos_a = jnp.concatenate([rc, rc, cc, cc] * 2, axis=-1)
    sin_a = jnp.concatenate([-rs, rs, -cs, cs] * 2, axis=-1)
    cos_d = jnp.concatenate([tc, tc] * 2, axis=-1)
    sin_d = jnp.concatenate([-ts, ts] * 2, axis=-1)
    return cos_a, sin_a, cos_d, sin_d


def _trunk(x, params, tm, gqa_rows, diff_rows):
    batch, seq, _ = x.shape
    tokens = batch * seq
    (w_in, w_out, attn_norm, qn, kn, lq1, lk1, lq2, lk2, sub_norm, ffn_norm, w_gate_up, w_down,
     final_norm) = params
    depth = w_in.shape[0]
    tables = _rope_tables(seq)
    x2d = x.reshape(tokens, D_MODEL)
    for l in range(depth):
        lambda_init = 0.8 - 0.6 * math.exp(-0.3 * l)
        qg, kg, vg, qd, kd, vd = _proj_call(x2d, seq, attn_norm[l], w_in[l], qn[l], kn[l], *tables, tm)
        shape3 = lambda a: a.reshape(batch, seq, a.shape[-1])
        og = _gqa_call(shape3(qg), shape3(kg), shape3(vg), gqa_rows)
        od = _diff_call(shape3(qd), shape3(kd), shape3(vd), lq1[l], lk1[l], lq2[l], lk2[l],
                        sub_norm[l], lambda_init, diff_rows)
        x2d = _mlp_call(x2d, og.reshape(tokens, GQA_Q), od.reshape(tokens, DIFF_V), w_out[l], ffn_norm[l],
                        w_gate_up[l], w_down[l], final_norm, l == depth - 1, tm)
    return x2d.reshape(batch, seq, D_MODEL)


def kernel(x_prompt, x_sample, w_in, w_out, attn_norm, gqa_q_norm, gqa_k_norm, diff_lambda_q1, diff_lambda_k1, diff_lambda_q2, diff_lambda_k2, diff_sub_norm, ffn_norm, w_gate_up, w_down, final_norm):
    depth = w_in.shape[0]
    row = lambda a: a.reshape(depth, 1, a.shape[-1])
    tile2 = lambda a: jnp.tile(a, (1, LANES // HEAD_DIM)).reshape(depth, 1, LANES)
    params = (
        w_in.astype(BF16), w_out.astype(BF16), row(attn_norm), tile2(gqa_q_norm), tile2(gqa_k_norm),
        row(diff_lambda_q1), row(diff_lambda_k1), row(diff_lambda_q2), row(diff_lambda_k2),
        row(diff_sub_norm), row(ffn_norm), w_gate_up.astype(BF16), w_down.astype(BF16),
        final_norm.reshape(1, D_MODEL),
    )
    y_prompt = _trunk(x_prompt, params, tm=512, gqa_rows=128, diff_rows=256)
    y_sample = _trunk(x_sample, params, tm=512, gqa_rows=128, diff_rows=256)
    return (y_prompt, y_sample)
```

```python
import functools
import math

import jax
import jax.numpy as jnp
from jax import lax
from jax.experimental import pallas as pl
from jax.experimental.pallas import tpu as pltpu

D_MODEL = 1024
GRID_W = 64
HEAD_DIM = 64
AXIAL_DIM = HEAD_DIM // 2
GQA_HEADS = 8
GQA_KV_HEADS = 2
DIFF_HEADS = 4
DIFF_V_DIM = 2 * HEAD_DIM
GQA_Q = GQA_HEADS * HEAD_DIM
GQA_KV = GQA_KV_HEADS * HEAD_DIM
DIFF_QK = DIFF_HEADS * 2 * HEAD_DIM
DIFF_V = DIFF_HEADS * DIFF_V_DIM
IN_WIDTH = GQA_Q + 2 * GQA_KV + 2 * DIFF_QK + DIFF_V
D_FF = 2816
ROPE_THETA = 10000.0
NORM_EPS = 1e-6
DIFF_NORM_EPS = 1e-5

LANES = 128
VMEM_LIMIT_BYTES = 56 * 1024 * 1024
Q_PRESCALE = (HEAD_DIM ** -0.5) * math.log2(math.e)

OFF_QG = 0
OFF_KG = GQA_Q
OFF_VG = OFF_KG + GQA_KV
OFF_QD = OFF_VG + GQA_KV
OFF_KD = OFF_QD + DIFF_QK
OFF_VD = OFF_KD + DIFF_QK

BF16 = jnp.bfloat16
F32 = jnp.float32


def _dot(a, b):
    return jnp.dot(a, b, preferred_element_type=F32)


def _dot_nt(a, b):
    return lax.dot_general(a, b, (((1,), (1,)), ((), ())), preferred_element_type=F32)


def _rms(x, gain, eps):
    ms = jnp.mean(x * x, axis=-1, keepdims=True)
    return x * lax.rsqrt(ms + eps) * gain


def _proj_body(x_ref, an_ref, w_ref, qn_ref, kn_ref, ca_ref, sa_ref, cd_ref, sd_ref,
               qg_ref, kg_ref, vg_ref, qd_ref, kd_ref, vd_ref):
    tm = x_ref.shape[0]
    h = _rms(x_ref[...], an_ref[...], NORM_EPS).astype(BF16)
    proj = _dot(h, w_ref[...])

    lane = lax.broadcasted_iota(jnp.int32, (tm, LANES), 1)
    low_half = lane < HEAD_DIM
    r = (lax.broadcasted_iota(jnp.int32, (2 * LANES, LANES), 0) % LANES) // HEAD_DIM
    c = lax.broadcasted_iota(jnp.int32, (2 * LANES, LANES), 1) // HEAD_DIM
    blk = (r == c).astype(BF16)

    def head_norm(xc, gain):
        y = xc * xc
        hi = y.astype(BF16)
        lo = (y - hi.astype(F32)).astype(BF16)
        ss = _dot(jnp.concatenate([hi, lo], axis=1), blk)
        return xc * lax.rsqrt(ss * (1.0 / HEAD_DIM) + NORM_EPS) * gain

    def rope(xc, cos, sin_signed, half):
        fwd = pltpu.roll(xc, LANES - half, 1)
        bwd = pltpu.roll(xc, half, 1)
        partner = jnp.where((lane % (2 * half)) < half, fwd, bwd)
        return xc * cos + partner * sin_signed

    ca, sa, cd, sd = ca_ref[...], sa_ref[...], cd_ref[...], sd_ref[...]
    qn, kn = qn_ref[...], kn_ref[...]

    for j in range(GQA_Q // LANES):
        xc = proj[:, OFF_QG + j * LANES: OFF_QG + (j + 1) * LANES]
        xc = rope(head_norm(xc, qn), ca, sa, AXIAL_DIM // 2) * Q_PRESCALE
        qg_ref[:, j * LANES:(j + 1) * LANES] = xc.astype(BF16)

    kc = proj[:, OFF_KG:OFF_KG + LANES]
    kc = rope(head_norm(kc, kn), ca, sa, AXIAL_DIM // 2)
    kc_sw = pltpu.roll(kc, HEAD_DIM, 1)
    zero = jnp.zeros_like(kc)
    kg_ref[:, 0 * LANES:1 * LANES] = jnp.where(low_half, kc, zero).astype(BF16)
    kg_ref[:, 1 * LANES:2 * LANES] = jnp.where(low_half, zero, kc_sw).astype(BF16)
    kg_ref[:, 2 * LANES:3 * LANES] = jnp.where(low_half, kc_sw, zero).astype(BF16)
    kg_ref[:, 3 * LANES:4 * LANES] = jnp.where(low_half, zero, kc).astype(BF16)

    vc = proj[:, OFF_VG:OFF_VG + LANES]
    vc_sw = pltpu.roll(vc, HEAD_DIM, 1)
    one = jnp.ones_like(vc)
    vg_ref[:, 0 * LANES:1 * LANES] = jnp.where(low_half, vc, one).astype(BF16)
    vg_ref[:, 1 * LANES:2 * LANES] = jnp.where(low_half, vc_sw, one).astype(BF16)

    for j in range(DIFF_HEADS):
        xc = proj[:, OFF_QD + j * LANES: OFF_QD + (j + 1) * LANES]
        qd_ref[:, j * LANES:(j + 1) * LANES] = (rope(xc, cd, sd, HEAD_DIM // 2) * Q_PRESCALE).astype(BF16)
        kc = rope(proj[:, OFF_KD + j * LANES: OFF_KD + (j + 1) * LANES], cd, sd, HEAD_DIM // 2)
        zero = jnp.zeros_like(kc)
        kd_ref[:, (2 * j) * LANES:(2 * j + 1) * LANES] = jnp.where(low_half, kc, zero).astype(BF16)
        kd_ref[:, (2 * j + 1) * LANES:(2 * j + 2) * LANES] = jnp.where(low_half, zero, kc).astype(BF16)
        vc = proj[:, OFF_VD + j * LANES: OFF_VD + (j + 1) * LANES]
        vd_ref[:, (2 * j) * LANES:(2 * j + 1) * LANES] = vc.astype(BF16)
        vd_ref[:, (2 * j + 1) * LANES:(2 * j + 2) * LANES] = jnp.ones_like(vc).astype(BF16)


def _proj_call(x2d, seq, attn_norm, w_in, qn, kn, ca, sa, cd, sd, tm):
    tokens = x2d.shape[0]
    n_pos = seq // tm
    tok = lambda i: (i, 0)
    pos = lambda i: (i % n_pos, 0)
    const = lambda i: (0, 0)
    resident = functools.partial(pl.BlockSpec, index_map=const, pipeline_mode=pl.Buffered(1))
    widths = (GQA_Q, 4 * LANES, 2 * LANES, DIFF_QK, 2 * DIFF_QK, 2 * DIFF_V)
    return pl.pallas_call(
        _proj_body,
        grid=(tokens // tm,),
        in_specs=[
            pl.BlockSpec((tm, D_MODEL), tok),
            resident((1, D_MODEL)),
            resident((D_MODEL, IN_WIDTH)),
            resident((1, LANES)),
            resident((1, LANES)),
            pl.BlockSpec((tm, LANES), pos),
            pl.BlockSpec((tm, LANES), pos),
            pl.BlockSpec((tm, LANES), pos),
            pl.BlockSpec((tm, LANES), pos),
        ],
        out_specs=[pl.BlockSpec((tm, w), tok) for w in widths],
        out_shape=[jax.ShapeDtypeStruct((tokens, w), BF16) for w in widths],
        compiler_params=pltpu.CompilerParams(
            dimension_semantics=("parallel",), vmem_limit_bytes=VMEM_LIMIT_BYTES),
        name="proj",
    )(x2d, attn_norm, w_in, qn, kn, ca, sa, cd, sd)


def _softmax_pv(s, v):
    m = jnp.max(s, axis=-1, keepdims=True)
    p = jnp.exp2(s - m).astype(BF16)
    return _dot(p, v)


UNROLLED_SCORE_COLUMNS = 8192


def _pipelined_rows(n_blocks, seq, scores, consume):
    per_trip = max(2, UNROLLED_SCORE_COLUMNS // seq)
    assert per_trip % 2 == 0 and n_blocks >= 2
    scores(0, 0)
    looped = ((n_blocks - 1) // per_trip) * per_trip

    def trip(t, carry):
        for u in range(per_trip):
            i = t * per_trip + u
            scores(i + 1, (u + 1) % 2)
            consume(i, u % 2)
        return carry

    lax.fori_loop(0, looped // per_trip, trip, 0)
    for i in range(looped, n_blocks):
        if i + 1 < n_blocks:
            scores(i + 1, (i + 1) % 2)
        consume(i, i % 2)


def _gqa_body(rows, q_ref, k_ref, v_ref, o_ref, s_scr0, s_scr1):
    s_scr = (s_scr0, s_scr1)
    seq = q_ref.shape[0]
    low_half = lax.broadcasted_iota(jnp.int32, (rows, LANES), 1) < HEAD_DIM

    def scores(i, slot):
        q = q_ref[pl.ds(pl.multiple_of(i * rows, rows), rows), :]
        q2 = jnp.concatenate([q[:, :LANES], q[:, LANES:]], axis=0)
        s_scr[slot][0] = _dot_nt(q2, k_ref[:, :LANES])
        s_scr[slot][1] = _dot_nt(q2, k_ref[:, LANES:])

    def consume(i, slot):
        v = v_ref[...]
        r_lo = _softmax_pv(s_scr[slot][0], v)
        r_hi = _softmax_pv(s_scr[slot][1], v)
        cols = []
        for col in range(2):
            lo = r_lo[col * rows:(col + 1) * rows]
            hi = r_hi[col * rows:(col + 1) * rows]
            lo_sw = pltpu.roll(lo, HEAD_DIM, 1)
            hi_sw = pltpu.roll(hi, HEAD_DIM, 1)
            cols.append(jnp.where(low_half, lo / lo_sw, hi_sw / hi))
        out = jnp.concatenate(cols, axis=1)
        o_ref[pl.ds(pl.multiple_of(i * rows, rows), rows), :] = out.astype(o_ref.dtype)

    _pipelined_rows(seq // rows, seq, scores, consume)


def _gqa_call(qg, kg, vg, rows):
    batch, seq, _ = qg.shape
    return pl.pallas_call(
        functools.partial(_gqa_body, rows),
        grid=(batch, GQA_KV_HEADS),
        in_specs=[
            pl.BlockSpec((None, seq, 2 * LANES), lambda b, g: (b, 0, g)),
            pl.BlockSpec((None, seq, 2 * LANES), lambda b, g: (b, 0, g)),
            pl.BlockSpec((None, seq, LANES), lambda b, g: (b, 0, g)),
        ],
        out_specs=pl.BlockSpec((None, seq, 2 * LANES), lambda b, g: (b, 0, g)),
        out_shape=jax.ShapeDtypeStruct((batch, seq, GQA_Q), BF16),
        scratch_shapes=[pltpu.VMEM((2, 2 * rows, seq), F32)] * 2,
        compiler_params=pltpu.CompilerParams(
            dimension_semantics=("parallel", "parallel"),
            vmem_limit_bytes=VMEM_LIMIT_BYTES),
        name="gqa_attn",
    )(qg, kg, vg)


def _diff_body(lambda_init, rows, q_ref, k_ref, v_ref, lq1_ref, lk1_ref, lq2_ref, lk2_ref, sn_ref, o_ref,
               s_scr0, s_scr1):
    s_scr = (s_scr0, s_scr1)
    seq = q_ref.shape[0]
    lam = (jnp.exp(jnp.sum(lq1_ref[...] * lk1_ref[...], axis=-1, keepdims=True))
           - jnp.exp(jnp.sum(lq2_ref[...] * lk2_ref[...], axis=-1, keepdims=True)) + lambda_init)

    def scores(i, slot):
        q = q_ref[pl.ds(pl.multiple_of(i * rows, rows), rows), :]
        s_scr[slot][0] = _dot_nt(q, k_ref[:, :LANES])
        s_scr[slot][1] = _dot_nt(q, k_ref[:, LANES:])

    def consume(i, slot):
        v = v_ref[...]
        r1 = _softmax_pv(s_scr[slot][0], v)
        r2 = _softmax_pv(s_scr[slot][1], v)
        o = r1[:, :LANES] / r1[:, LANES:] - lam * (r2[:, :LANES] / r2[:, LANES:])
        o = _rms(o, sn_ref[...], DIFF_NORM_EPS) * (1.0 - lambda_init)
        o_ref[pl.ds(pl.multiple_of(i * rows, rows), rows), :] = o.astype(o_ref.dtype)

    _pipelined_rows(seq // rows, seq, scores, consume)


def _diff_call(qd, kd, vd, lq1, lk1, lq2, lk2, sub_norm, lambda_init, rows):
    batch, seq, _ = qd.shape
    const = lambda b, h: (0, 0)
    return pl.pallas_call(
        functools.partial(_diff_body, lambda_init, rows),
        grid=(batch, DIFF_HEADS),
        in_specs=[
            pl.BlockSpec((None, seq, LANES), lambda b, h: (b, 0, h)),
            pl.BlockSpec((None, seq, 2 * LANES), lambda b, h: (b, 0, h)),
            pl.BlockSpec((None, seq, 2 * LANES), lambda b, h: (b, 0, h)),
            pl.BlockSpec((1, HEAD_DIM), const),
            pl.BlockSpec((1, HEAD_DIM), const),
            pl.BlockSpec((1, HEAD_DIM), const),
            pl.BlockSpec((1, HEAD_DIM), const),
            pl.BlockSpec((1, DIFF_V_DIM), const),
        ],
        out_specs=pl.BlockSpec((None, seq, LANES), lambda b, h: (b, 0, h)),
        out_shape=jax.ShapeDtypeStruct((batch, seq, DIFF_V), BF16),
        scratch_shapes=[pltpu.VMEM((2, rows, seq), F32)] * 2,
        compiler_params=pltpu.CompilerParams(
            dimension_semantics=("parallel", "parallel"),
            vmem_limit_bytes=VMEM_LIMIT_BYTES),
        name="diff_attn",
    )(qd, kd, vd, lq1, lk1, lq2, lk2, sub_norm)


MXU_TILE = 256
FF_SPLITS = (0, 6 * MXU_TILE, D_FF)


def _mlp_body(apply_final_norm, x_ref, og_ref, od_ref, wo_ref, fn_ref, wgu_ref, wd_ref, final_ref, y_ref):
    x = x_ref[...] + (_dot(og_ref[...], wo_ref[:GQA_Q, :]) + _dot(od_ref[...], wo_ref[GQA_Q:, :]))
    h = _rms(x, fn_ref[...], NORM_EPS).astype(BF16)
    ffn = None
    for c0, c1 in zip(FF_SPLITS[:-1], FF_SPLITS[1:]):
        gate = _dot(h, wgu_ref[:, c0:c1])
        up = _dot(h, wgu_ref[:, D_FF + c0:D_FF + c1])
        act = (gate * jax.nn.sigmoid(gate) * up).astype(BF16)
        down = _dot(act, wd_ref[c0:c1, :])
        ffn = down if ffn is None else ffn + down
    x = x + ffn
    if apply_final_norm:
        x = _rms(x, final_ref[...], NORM_EPS)
    y_ref[...] = x


def _mlp_call(x2d, og, od, w_out, ffn_norm, w_gate_up, w_down, final_norm, apply_final_norm, tm):
    tokens = x2d.shape[0]
    tok = lambda i: (i, 0)
    const = lambda i: (0, 0)
    resident = functools.partial(pl.BlockSpec, index_map=const, pipeline_mode=pl.Buffered(1))
    return pl.pallas_call(
        functools.partial(_mlp_body, apply_final_norm),
        grid=(tokens // tm,),
        in_specs=[
            pl.BlockSpec((tm, D_MODEL), tok),
            pl.BlockSpec((tm, GQA_Q), tok),
            pl.BlockSpec((tm, DIFF_V), tok),
            resident((GQA_Q + DIFF_V, D_MODEL)),
            resident((1, D_MODEL)),
            resident((D_MODEL, 2 * D_FF)),
            resident((D_FF, D_MODEL)),
            resident((1, D_MODEL)),
        ],
        out_specs=pl.BlockSpec((tm, D_MODEL), tok),
        out_shape=jax.ShapeDtypeStruct((tokens, D_MODEL), F32),
        compiler_params=pltpu.CompilerParams(
            dimension_semantics=("parallel",), vmem_limit_bytes=VMEM_LIMIT_BYTES),
        name="mlp",
    )(x2d, og, od, w_out, ffn_norm, w_gate_up, w_down, final_norm)


def _rope_tables(seq):
    def angles(pos, dim):
        inv = ROPE_THETA ** (-jnp.arange(0, dim, 2, dtype=F32) / dim)
        ang = pos.astype(F32)[:, None] * inv[None, :]
        return jnp.cos(ang), jnp.sin(ang)

    rows = seq // GRID_W
    row = jnp.repeat(jnp.arange(rows, dtype=jnp.int32), GRID_W)
    col = jnp.tile(jnp.arange(GRID_W, dtype=jnp.int32), rows)
    t = jnp.arange(seq, dtype=jnp.int32)
    rc, rs = angles(row, AXIAL_DIM)
    cc, cs = angles(col, AXIAL_DIM)
    tc, ts = angles(t, HEAD_DIM)
    cos_a = jnp.concatenate([rc, rc, cc, cc] * 2, axis=-1)
    sin_a = jnp.concatenate([-rs, rs, -cs, cs] * 2, axis=-1)
    cos_d = jnp.concatenate([tc, tc] * 2, axis=-1)
    sin_d = jnp.concatenate([-ts, ts] * 2, axis=-1)
    return cos_a, sin_a, cos_d, sin_d


def _trunk(x, params, tm, gqa_rows, diff_rows):
    batch, seq, _ = x.shape
    tokens = batch * seq
    (w_in, w_out, attn_norm, qn, kn, lq1, lk1, lq2, lk2, sub_norm, ffn_norm, w_gate_up, w_down,
     final_norm) = params
    depth = w_in.shape[0]
    tables = _rope_tables(seq)
    x2d = x.reshape(tokens, D_MODEL)
    for l in range(depth):
        lambda_init = 0.8 - 0.6 * math.exp(-0.3 * l)
        qg, kg, vg, qd, kd, vd = _proj_call(x2d, seq, attn_norm[l], w_in[l], qn[l], kn[l], *tables, tm)
        shape3 = lambda a: a.reshape(batch, seq, a.shape[-1])
        og = _gqa_call(shape3(qg), shape3(kg), shape3(vg), gqa_rows)
        od = _diff_call(shape3(qd), shape3(kd), shape3(vd), lq1[l], lk1[l], lq2[l], lk2[l],
                        sub_norm[l], lambda_init, diff_rows)
        x2d = _mlp_call(x2d, og.reshape(tokens, GQA_Q), od.reshape(tokens, DIFF_V), w_out[l], ffn_norm[l],
                        w_gate_up[l], w_down[l], final_norm, l == depth - 1, tm)
    return x2d.reshape(batch, seq, D_MODEL)


def kernel(x_prompt, x_sample, w_in, w_out, attn_norm, gqa_q_norm, gqa_k_norm, diff_lambda_q1, diff_lambda_k1, diff_lambda_q2, diff_lambda_k2, diff_sub_norm, ffn_norm, w_gate_up, w_down, final_norm):
    depth = w_in.shape[0]
    row = lambda a: a.reshape(depth, 1, a.shape[-1])
    tile2 = lambda a: jnp.tile(a, (1, LANES // HEAD_DIM)).reshape(depth, 1, LANES)
    params = (
        w_in.astype(BF16), w_out.astype(BF16), row(attn_norm), tile2(gqa_q_norm), tile2(gqa_k_norm),
        row(diff_lambda_q1), row(diff_lambda_k1), row(diff_lambda_q2), row(diff_lambda_k2),
        row(diff_sub_norm), row(ffn_norm), w_gate_up.astype(BF16), w_down.astype(BF16),
        final_norm.reshape(1, D_MODEL),
    )
    y_prompt = _trunk(x_prompt, params, tm=512, gqa_rows=128, diff_rows=256)
    y_sample = _trunk(x_sample, params, tm=512, gqa_rows=128, diff_rows=256)
    return (y_prompt, y_sample)
```

```python
import functools
import math

import jax
import jax.numpy as jnp
from jax import lax
from jax.experimental import pallas as pl
from jax.experimental.pallas import tpu as pltpu

D_MODEL = 1024
GRID_W = 64
HEAD_DIM = 64
AXIAL_DIM = HEAD_DIM // 2
GQA_HEADS = 8
GQA_KV_HEADS = 2
DIFF_HEADS = 4
DIFF_V_DIM = 2 * HEAD_DIM
GQA_Q = GQA_HEADS * HEAD_DIM
GQA_KV = GQA_KV_HEADS * HEAD_DIM
DIFF_QK = DIFF_HEADS * 2 * HEAD_DIM
DIFF_V = DIFF_HEADS * DIFF_V_DIM
IN_WIDTH = GQA_Q + 2 * GQA_KV + 2 * DIFF_QK + DIFF_V
D_FF = 2816
ROPE_THETA = 10000.0
NORM_EPS = 1e-6
DIFF_NORM_EPS = 1e-5

LANES = 128
PROJ_ROW_CHAINS = 2
VMEM_LIMIT_BYTES = 56 * 1024 * 1024
Q_PRESCALE = (HEAD_DIM ** -0.5) * math.log2(math.e)

OFF_QG = 0
OFF_KG = GQA_Q
OFF_VG = OFF_KG + GQA_KV
OFF_QD = OFF_VG + GQA_KV
OFF_KD = OFF_QD + DIFF_QK
OFF_VD = OFF_KD + DIFF_QK

BF16 = jnp.bfloat16
F32 = jnp.float32


def _dot(a, b):
    return jnp.dot(a, b, preferred_element_type=F32)


def _dot_nt(a, b):
    return lax.dot_general(a, b, (((1,), (1,)), ((), ())), preferred_element_type=F32)


def _rms(x, gain, eps):
    ms = jnp.mean(x * x, axis=-1, keepdims=True)
    return x * lax.rsqrt(ms + eps) * gain


def _proj_body(x_ref, an_ref, w_ref, qn_ref, kn_ref, ca_ref, sa_ref, cd_ref, sd_ref,
               qg_ref, kg_ref, vg_ref, qd_ref, kd_ref, vd_ref):
    tm = x_ref.shape[0] // PROJ_ROW_CHAINS
    lane = lax.broadcasted_iota(jnp.int32, (tm, LANES), 1)
    low_half = lane < HEAD_DIM
    first_slot = (lane % HEAD_DIM) < (HEAD_DIM // 2)
    r = (lax.broadcasted_iota(jnp.int32, (2 * LANES, LANES), 0) % HEAD_DIM) // (HEAD_DIM // 2)
    c = (lax.broadcasted_iota(jnp.int32, (2 * LANES, LANES), 1) % HEAD_DIM) // (HEAD_DIM // 2)
    blk = (r == c).astype(BF16)

    def head_norm(xc, gain):
        y = xc * xc
        hi = y.astype(BF16)
        lo = (y - hi.astype(F32)).astype(BF16)
        ss = _dot(jnp.concatenate([hi, lo], axis=1), blk)
        return xc * lax.rsqrt(ss * (1.0 / HEAD_DIM) + NORM_EPS) * gain

    def rope(xc, cos, sin_signed):
        return xc * cos + pltpu.roll(xc, HEAD_DIM, 1) * sin_signed

    zero = jnp.zeros((tm, LANES), F32)
    one = jnp.ones((tm, LANES), F32)

    def _proj_rows(rows):
        h = _rms(x_ref[rows, :], an_ref[...], NORM_EPS).astype(BF16)
        proj = _dot(h, w_ref[...])
        ca, sa, cd, sd = ca_ref[rows, :], sa_ref[rows, :], cd_ref[rows, :], sd_ref[rows, :]

        def project(offset, width):
            return proj[:, offset:offset + width]

        qg = project(OFF_QG, GQA_Q)
        for j in range(GQA_Q // LANES):
            xc = rope(head_norm(qg[:, j * LANES:(j + 1) * LANES], qn_ref[...]), ca, sa) * Q_PRESCALE
            qg_ref[rows, j * LANES:(j + 1) * LANES] = xc.astype(BF16)

        kv = project(OFF_KG, 2 * GQA_KV)
        kc = rope(head_norm(kv[:, :LANES], kn_ref[...]), ca, sa)
        to_first = pltpu.roll(kc, LANES - HEAD_DIM // 2, 1)
        to_second = pltpu.roll(kc, HEAD_DIM // 2, 1)
        kg_ref[rows, 0 * LANES:1 * LANES] = jnp.where(first_slot, kc, zero).astype(BF16)
        kg_ref[rows, 1 * LANES:2 * LANES] = jnp.where(first_slot, zero, to_second).astype(BF16)
        kg_ref[rows, 2 * LANES:3 * LANES] = jnp.where(first_slot, to_first, zero).astype(BF16)
        kg_ref[rows, 3 * LANES:4 * LANES] = jnp.where(first_slot, zero, kc).astype(BF16)

        vc = kv[:, LANES:]
        vg_ref[rows, 0 * LANES:1 * LANES] = jnp.where(low_half, vc, one).astype(BF16)
        vg_ref[rows, 1 * LANES:2 * LANES] = jnp.where(low_half, pltpu.roll(vc, HEAD_DIM, 1), one).astype(BF16)

        qd = project(OFF_QD, DIFF_QK)
        for j in range(DIFF_HEADS):
            xc = rope(qd[:, j * LANES:(j + 1) * LANES], cd, sd) * Q_PRESCALE
            qd_ref[rows, j * LANES:(j + 1) * LANES] = xc.astype(BF16)
        kd = project(OFF_KD, DIFF_QK)
        for j in range(DIFF_HEADS):
            kc = rope(kd[:, j * LANES:(j + 1) * LANES], cd, sd)
            kd_ref[rows, (2 * j) * LANES:(2 * j + 1) * LANES] = jnp.where(first_slot, kc, zero).astype(BF16)
            kd_ref[rows, (2 * j + 1) * LANES:(2 * j + 2) * LANES] = jnp.where(first_slot, zero, kc).astype(BF16)
        vd = project(OFF_VD, DIFF_V)
        for j in range(DIFF_HEADS):
            vd_ref[rows, (2 * j) * LANES:(2 * j + 1) * LANES] = vd[:, j * LANES:(j + 1) * LANES].astype(BF16)
            vd_ref[rows, (2 * j + 1) * LANES:(2 * j + 2) * LANES] = one.astype(BF16)

    for chain in range(PROJ_ROW_CHAINS):
        _proj_rows(pl.ds(chain * tm, tm))


def _proj_call(x2d, seq, attn_norm, w_in, qn, kn, ca, sa, cd, sd, tm):
    tokens = x2d.shape[0]
    n_pos = seq // tm
    tok = lambda i: (i, 0)
    pos = lambda i: (i % n_pos, 0)
    const = lambda i: (0, 0)
    resident = functools.partial(pl.BlockSpec, index_map=const, pipeline_mode=pl.Buffered(1))
    widths = (GQA_Q, 4 * LANES, 2 * LANES, DIFF_QK, 2 * DIFF_QK, 2 * DIFF_V)
    return pl.pallas_call(
        _proj_body,
        grid=(tokens // tm,),
        in_specs=[
            pl.BlockSpec((tm, D_MODEL), tok),
            resident((1, D_MODEL)),
            resident((D_MODEL, IN_WIDTH)),
            resident((1, LANES)),
            resident((1, LANES)),
            pl.BlockSpec((tm, LANES), pos),
            pl.BlockSpec((tm, LANES), pos),
            pl.BlockSpec((tm, LANES), pos),
            pl.BlockSpec((tm, LANES), pos),
        ],
        out_specs=[pl.BlockSpec((tm, w), tok) for w in widths],
        out_shape=[jax.ShapeDtypeStruct((tokens, w), BF16) for w in widths],
        compiler_params=pltpu.CompilerParams(
            dimension_semantics=("parallel",), vmem_limit_bytes=VMEM_LIMIT_BYTES),
        name="proj",
    )(x2d, attn_norm, w_in, qn, kn, ca, sa, cd, sd)


def _softmax_pv(s, v):
    m = jnp.max(s, axis=-1, keepdims=True)
    p = jnp.exp2(s - m).astype(BF16)
    return _dot(p, v)


UNROLLED_SCORE_COLUMNS = 8192


def _pipelined_rows(n_blocks, seq, scores, consume):
    per_trip = max(2, UNROLLED_SCORE_COLUMNS // seq)
    assert per_trip % 2 == 0 and n_blocks >= 2
    scores(0, 0)
    looped = ((n_blocks - 1) // per_trip) * per_trip

    def trip(t, carry):
        for u in range(per_trip):
            i = t * per_trip + u
            scores(i + 1, (u + 1) % 2)
            consume(i, u % 2)
        return carry

    lax.fori_loop(0, looped // per_trip, trip, 0)
    for i in range(looped, n_blocks):
        if i + 1 < n_blocks:
            scores(i + 1, (i + 1) % 2)
        consume(i, i % 2)


def _gqa_body(rows, q_ref, k_ref, v_ref, o_ref, s_scr0, s_scr1):
    s_scr = (s_scr0, s_scr1)
    seq = q_ref.shape[0]
    low_half = lax.broadcasted_iota(jnp.int32, (rows, LANES), 1) < HEAD_DIM

    def scores(i, slot):
        q = q_ref[pl.ds(pl.multiple_of(i * rows, rows), rows), :]
        q2 = jnp.concatenate([q[:, :LANES], q[:, LANES:]], axis=0)
        s_scr[slot][0] = _dot_nt(q2, k_ref[:, :LANES])
        s_scr[slot][1] = _dot_nt(q2, k_ref[:, LANES:])

    def consume(i, slot):
        v = v_ref[...]
        r_lo = _softmax_pv(s_scr[slot][0], v)
        r_hi = _softmax_pv(s_scr[slot][1], v)
        cols = []
        for col in range(2):
            lo = r_lo[col * rows:(col + 1) * rows]
            hi = r_hi[col * rows:(col + 1) * rows]
            lo_sw = pltpu.roll(lo, HEAD_DIM, 1)
            hi_sw = pltpu.roll(hi, HEAD_DIM, 1)
            cols.append(jnp.where(low_half, lo / lo_sw, hi_sw / hi))
        out = jnp.concatenate(cols, axis=1)
        o_ref[pl.ds(pl.multiple_of(i * rows, rows), rows), :] = out.astype(o_ref.dtype)

    _pipelined_rows(seq // rows, seq, scores, consume)


def _gqa_call(qg, kg, vg, rows):
    batch, seq, _ = qg.shape
    return pl.pallas_call(
        functools.partial(_gqa_body, rows),
        grid=(batch, GQA_KV_HEADS),
        in_specs=[
            pl.BlockSpec((None, seq, 2 * LANES), lambda b, g: (b, 0, g)),
            pl.BlockSpec((None, seq, 2 * LANES), lambda b, g: (b, 0, g)),
            pl.BlockSpec((None, seq, LANES), lambda b, g: (b, 0, g)),
        ],
        out_specs=pl.BlockSpec((None, seq, 2 * LANES), lambda b, g: (b, 0, g)),
        out_shape=jax.ShapeDtypeStruct((batch, seq, GQA_Q), BF16),
        scratch_shapes=[pltpu.VMEM((2, 2 * rows, seq), F32)] * 2,
        compiler_params=pltpu.CompilerParams(
            dimension_semantics=("parallel", "parallel"),
            vmem_limit_bytes=VMEM_LIMIT_BYTES),
        name="gqa_attn",
    )(qg, kg, vg)


def _diff_body(lambda_init, rows, q_ref, k_ref, v_ref, lq1_ref, lk1_ref, lq2_ref, lk2_ref, sn_ref, o_ref,
               s_scr0, s_scr1):
    s_scr = (s_scr0, s_scr1)
    seq = q_ref.shape[0]
    lam = (jnp.exp(jnp.sum(lq1_ref[...] * lk1_ref[...], axis=-1, keepdims=True))
           - jnp.exp(jnp.sum(lq2_ref[...] * lk2_ref[...], axis=-1, keepdims=True)) + lambda_init)

    def scores(i, slot):
        q = q_ref[pl.ds(pl.multiple_of(i * rows, rows), rows), :]
        s_scr[slot][0] = _dot_nt(q, k_ref[:, :LANES])
        s_scr[slot][1] = _dot_nt(q, k_ref[:, LANES:])

    def consume(i, slot):
        v = v_ref[...]
        r1 = _softmax_pv(s_scr[slot][0], v)
        r2 = _softmax_pv(s_scr[slot][1], v)
        o = r1[:, :LANES] / r1[:, LANES:] - lam * (r2[:, :LANES] / r2[:, LANES:])
        o = _rms(o, sn_ref[...], DIFF_NORM_EPS) * (1.0 - lambda_init)
        o_ref[pl.ds(pl.multiple_of(i * rows, rows), rows), :] = o.astype(o_ref.dtype)

    _pipelined_rows(seq // rows, seq, scores, consume)


def _diff_call(qd, kd, vd, lq1, lk1, lq2, lk2, sub_norm, lambda_init, rows):
    batch, seq, _ = qd.shape
    const = lambda b, h: (0, 0)
    return pl.pallas_call(
        functools.partial(_diff_body, lambda_init, rows),
        grid=(batch, DIFF_HEADS),
        in_specs=[
            pl.BlockSpec((None, seq, LANES), lambda b, h: (b, 0, h)),
            pl.BlockSpec((None, seq, 2 * LANES), lambda b, h: (b, 0, h)),
            pl.BlockSpec((None, seq, 2 * LANES), lambda b, h: (b, 0, h)),
            pl.BlockSpec((1, HEAD_DIM), const),
            pl.BlockSpec((1, HEAD_DIM), const),
            pl.BlockSpec((1, HEAD_DIM), const),
            pl.BlockSpec((1, HEAD_DIM), const),
            pl.BlockSpec((1, DIFF_V_DIM), const),
        ],
        out_specs=pl.BlockSpec((None, seq, LANES), lambda b, h: (b, 0, h)),
        out_shape=jax.ShapeDtypeStruct((batch, seq, DIFF_V), BF16),
        scratch_shapes=[pltpu.VMEM((2, rows, seq), F32)] * 2,
        compiler_params=pltpu.CompilerParams(
            dimension_semantics=("parallel", "parallel"),
            vmem_limit_bytes=VMEM_LIMIT_BYTES),
        name="diff_attn",
    )(qd, kd, vd, lq1, lk1, lq2, lk2, sub_norm)


MXU_TILE = 256
FF_SPLITS = (0, 6 * MXU_TILE, D_FF)


def _mlp_body(apply_final_norm, x_ref, og_ref, od_ref, wo_ref, fn_ref, wgu_ref, wd_ref, final_ref, y_ref):
    x = x_ref[...] + (_dot(og_ref[...], wo_ref[:GQA_Q, :]) + _dot(od_ref[...], wo_ref[GQA_Q:, :]))
    h = _rms(x, fn_ref[...], NORM_EPS).astype(BF16)
    ffn = None
    for c0, c1 in zip(FF_SPLITS[:-1], FF_SPLITS[1:]):
        gate = _dot(h, wgu_ref[:, c0:c1])
        up = _dot(h, wgu_ref[:, D_FF + c0:D_FF + c1])
        act = (gate * jax.nn.sigmoid(gate) * up).astype(BF16)
        down = _dot(act, wd_ref[c0:c1, :])
        ffn = down if ffn is None else ffn + down
    x = x + ffn
    if apply_final_norm:
        x = _rms(x, final_ref[...], NORM_EPS)
    y_ref[...] = x


def _mlp_call(x2d, og, od, w_out, ffn_norm, w_gate_up, w_down, final_norm, apply_final_norm, tm):
    tokens = x2d.shape[0]
    tok = lambda i: (i, 0)
    const = lambda i: (0, 0)
    resident = functools.partial(pl.BlockSpec, index_map=const, pipeline_mode=pl.Buffered(1))
    return pl.pallas_call(
        functools.partial(_mlp_body, apply_final_norm),
        grid=(tokens // tm,),
        in_specs=[
            pl.BlockSpec((tm, D_MODEL), tok),
            pl.BlockSpec((tm, GQA_Q), tok),
            pl.BlockSpec((tm, DIFF_V), tok),
            resident((GQA_Q + DIFF_V, D_MODEL)),
            resident((1, D_MODEL)),
            resident((D_MODEL, 2 * D_FF)),
            resident((D_FF, D_MODEL)),
            resident((1, D_MODEL)),
        ],
        out_specs=pl.BlockSpec((tm, D_MODEL), tok),
        out_shape=jax.ShapeDtypeStruct((tokens, D_MODEL), F32),
        compiler_params=pltpu.CompilerParams(
            dimension_semantics=("parallel",), vmem_limit_bytes=VMEM_LIMIT_BYTES),
        name="mlp",
    )(x2d, og, od, w_out, ffn_norm, w_gate_up, w_down, final_norm)


def _rotary_lane_order():
    j = jnp.arange(LANES)
    half, jj = j // HEAD_DIM, j % HEAD_DIM
    slot = jj // (HEAD_DIM // 2)
    part, idx = (jj % 32) // 16, jj % 16
    gqa = slot * HEAD_DIM + part * AXIAL_DIM + half * (AXIAL_DIM // 2) + idx
    diff = slot * HEAD_DIM + half * (HEAD_DIM // 2) + jj % 32
    return gqa, diff


def _projection_column_order():
    gqa, diff = _rotary_lane_order()
    ident = jnp.arange(LANES)
    cols = []
    for off, n_cols, order in ((OFF_QG, GQA_Q // LANES, gqa), (OFF_KG, 1, gqa), (OFF_VG, 1, ident),
                               (OFF_QD, DIFF_HEADS, diff), (OFF_KD, DIFF_HEADS, diff),
                               (OFF_VD, DIFF_HEADS, ident)):
        for c in range(n_cols):
            cols.append(off + c * LANES + order)
    return jnp.concatenate(cols)


def _rope_tables(seq):
    def angles(pos, dim):
        inv = ROPE_THETA ** (-jnp.arange(0, dim, 2, dtype=F32) / dim)
        ang = pos.astype(F32)[:, None] * inv[None, :]
        return jnp.cos(ang), jnp.sin(ang)

    rows = seq // GRID_W
    row = jnp.repeat(jnp.arange(rows, dtype=jnp.int32), GRID_W)
    col = jnp.tile(jnp.arange(GRID_W, dtype=jnp.int32), rows)
    t = jnp.arange(seq, dtype=jnp.int32)
    rc, rs = angles(row, AXIAL_DIM)
    cc, cs = angles(col, AXIAL_DIM)
    tc, ts = angles(t, HEAD_DIM)
    cos_a = jnp.concatenate([rc, cc] * 4, axis=-1)
    sin_a = jnp.concatenate([-rs, -cs] * 2 + [rs, cs] * 2, axis=-1)
    cos_d = jnp.concatenate([tc] * 4, axis=-1)
    sin_d = jnp.concatenate([-ts] * 2 + [ts] * 2, axis=-1)
    return cos_a, sin_a, cos_d, sin_d


def _trunk(x, params, proj_tm, mlp_tm, gqa_rows, diff_rows):
    batch, seq, _ = x.shape
    tokens = batch * seq
    (w_in, w_out, attn_norm, qn, kn, lq1, lk1, lq2, lk2, sub_norm, ffn_norm, w_gate_up, w_down,
     final_norm) = params
    depth = w_in.shape[0]
    tables = _rope_tables(seq)
    x2d = x.reshape(tokens, D_MODEL)
    for l in range(depth):
        lambda_init = 0.8 - 0.6 * math.exp(-0.3 * l)
        qg, kg, vg, qd, kd, vd = _proj_call(x2d, seq, attn_norm[l], w_in[l], qn[l], kn[l], *tables, proj_tm)
        shape3 = lambda a: a.reshape(batch, seq, a.shape[-1])
        og = _gqa_call(shape3(qg), shape3(kg), shape3(vg), gqa_rows)
        od = _diff_call(shape3(qd), shape3(kd), shape3(vd), lq1[l], lk1[l], lq2[l], lk2[l],
                        sub_norm[l], lambda_init, diff_rows)
        x2d = _mlp_call(x2d, og.reshape(tokens, GQA_Q), od.reshape(tokens, DIFF_V), w_out[l], ffn_norm[l],
                        w_gate_up[l], w_down[l], final_norm, l == depth - 1, mlp_tm)
    return x2d.reshape(batch, seq, D_MODEL)


def _prepare_params(w_in, w_out, attn_norm, gqa_q_norm, gqa_k_norm, diff_lambda_q1, diff_lambda_k1,
                    diff_lambda_q2, diff_lambda_k2, diff_sub_norm, ffn_norm, w_gate_up, w_down, final_norm):
    depth = w_in.shape[0]
    row = lambda a: a.reshape(depth, 1, a.shape[-1])
    gqa_order, _ = _rotary_lane_order()
    tile2 = lambda a: jnp.tile(a, (1, LANES // HEAD_DIM))[:, gqa_order].reshape(depth, 1, LANES)
    return (
        w_in[:, :, _projection_column_order()].astype(BF16), w_out.astype(BF16), row(attn_norm),
        tile2(gqa_q_norm), tile2(gqa_k_norm),
        row(diff_lambda_q1), row(diff_lambda_k1), row(diff_lambda_q2), row(diff_lambda_k2),
        row(diff_sub_norm), row(ffn_norm), w_gate_up.astype(BF16), w_down.astype(BF16),
        final_norm.reshape(1, D_MODEL),
    )


def kernel(x_prompt, x_sample, w_in, w_out, attn_norm, gqa_q_norm, gqa_k_norm, diff_lambda_q1, diff_lambda_k1, diff_lambda_q2, diff_lambda_k2, diff_sub_norm, ffn_norm, w_gate_up, w_down, final_norm):
    params = _prepare_params(w_in, w_out, attn_norm, gqa_q_norm, gqa_k_norm, diff_lambda_q1, diff_lambda_k1,
                             diff_lambda_q2, diff_lambda_k2, diff_sub_norm, ffn_norm, w_gate_up, w_down,
                             final_norm)
    tiles = dict(proj_tm=1024, mlp_tm=512, gqa_rows=128, diff_rows=256)
    y_prompt = _trunk(x_prompt, params, **tiles)
    y_sample = _trunk(x_sample, params, **tiles)
    return (y_prompt, y_sample)
```

```python
import functools
import math

import jax
import jax.numpy as jnp
from jax import lax
from jax.experimental import pallas as pl
from jax.experimental.pallas import tpu as pltpu

D_MODEL = 1024
GRID_W = 64
HEAD_DIM = 64
AXIAL_DIM = HEAD_DIM // 2
GQA_HEADS = 8
GQA_KV_HEADS = 2
DIFF_HEADS = 4
DIFF_V_DIM = 2 * HEAD_DIM
GQA_Q = GQA_HEADS * HEAD_DIM
GQA_KV = GQA_KV_HEADS * HEAD_DIM
DIFF_QK = DIFF_HEADS * 2 * HEAD_DIM
DIFF_V = DIFF_HEADS * DIFF_V_DIM
IN_WIDTH = GQA_Q + 2 * GQA_KV + 2 * DIFF_QK + DIFF_V
D_FF = 2816
ROPE_THETA = 10000.0
NORM_EPS = 1e-6
DIFF_NORM_EPS = 1e-5

LANES = 128
PROJ_ROW_CHAINS = 2
VMEM_LIMIT_BYTES = 56 * 1024 * 1024
Q_PRESCALE = (HEAD_DIM ** -0.5) * math.log2(math.e)

OFF_QG = 0
OFF_KG = GQA_Q
OFF_VG = OFF_KG + GQA_KV
OFF_QD = OFF_VG + GQA_KV
OFF_KD = OFF_QD + DIFF_QK
OFF_VD = OFF_KD + DIFF_QK

BF16 = jnp.bfloat16
F32 = jnp.float32


def _dot(a, b):
    return jnp.dot(a, b, preferred_element_type=F32)


def _dot_nt(a, b):
    return lax.dot_general(a, b, (((1,), (1,)), ((), ())), preferred_element_type=F32)


def _rms(x, gain, eps):
    ms = jnp.mean(x * x, axis=-1, keepdims=True)
    return x * lax.rsqrt(ms + eps) * gain


def _proj_body(x_ref, an_ref, w_ref, qn_ref, kn_ref, ca_ref, sa_ref, cd_ref, sd_ref,
               qg_ref, kg_ref, vg_ref, qd_ref, kd_ref, vd_ref):
    tm = x_ref.shape[0] // PROJ_ROW_CHAINS
    lane = lax.broadcasted_iota(jnp.int32, (tm, LANES), 1)
    low_half = lane < HEAD_DIM
    first_slot = (lane % HEAD_DIM) < (HEAD_DIM // 2)
    r = (lax.broadcasted_iota(jnp.int32, (2 * LANES, LANES), 0) % HEAD_DIM) // (HEAD_DIM // 2)
    c = (lax.broadcasted_iota(jnp.int32, (2 * LANES, LANES), 1) % HEAD_DIM) // (HEAD_DIM // 2)
    blk = (r == c).astype(BF16)

    def head_norm(xc, gain):
        y = xc * xc
        hi = y.astype(BF16)
        lo = (y - hi.astype(F32)).astype(BF16)
        ss = _dot(jnp.concatenate([hi, lo], axis=1), blk)
        return xc * lax.rsqrt(ss * (1.0 / HEAD_DIM) + NORM_EPS) * gain

    def rope(xc, cos, sin_signed):
        return xc * cos + pltpu.roll(xc, HEAD_DIM, 1) * sin_signed

    zero = jnp.zeros((tm, LANES), F32)
    one = jnp.ones((tm, LANES), F32)

    def _proj_rows(rows):
        h = _rms(x_ref[rows, :], an_ref[...], NORM_EPS).astype(BF16)
        proj = _dot(h, w_ref[...])
        ca, sa, cd, sd = ca_ref[rows, :], sa_ref[rows, :], cd_ref[rows, :], sd_ref[rows, :]

        def project(offset, width):
            return proj[:, offset:offset + width]

        qg = project(OFF_QG, GQA_Q)
        for j in range(GQA_Q // LANES):
            xc = rope(head_norm(qg[:, j * LANES:(j + 1) * LANES], qn_ref[...]), ca, sa) * Q_PRESCALE
            qg_ref[rows, j * LANES:(j + 1) * LANES] = xc.astype(BF16)

        kv = project(OFF_KG, 2 * GQA_KV)
        kc = rope(head_norm(kv[:, :LANES], kn_ref[...]), ca, sa)
        to_first = pltpu.roll(kc, LANES - HEAD_DIM // 2, 1)
        to_second = pltpu.roll(kc, HEAD_DIM // 2, 1)
        kg_ref[rows, 0 * LANES:1 * LANES] = jnp.where(first_slot, kc, zero).astype(BF16)
        kg_ref[rows, 1 * LANES:2 * LANES] = jnp.where(first_slot, zero, to_second).astype(BF16)
        kg_ref[rows, 2 * LANES:3 * LANES] = jnp.where(first_slot, to_first, zero).astype(BF16)
        kg_ref[rows, 3 * LANES:4 * LANES] = jnp.where(first_slot, zero, kc).astype(BF16)

        vc = kv[:, LANES:]
        vg_ref[rows, 0 * LANES:1 * LANES] = jnp.where(low_half, vc, one).astype(BF16)
        vg_ref[rows, 1 * LANES:2 * LANES] = jnp.where(low_half, pltpu.roll(vc, HEAD_DIM, 1), one).astype(BF16)

        qd = project(OFF_QD, DIFF_QK)
        for j in range(DIFF_HEADS):
            xc = rope(qd[:, j * LANES:(j + 1) * LANES], cd, sd) * Q_PRESCALE
            qd_ref[rows, j * LANES:(j + 1) * LANES] = xc.astype(BF16)
        kd = project(OFF_KD, DIFF_QK)
        for j in range(DIFF_HEADS):
            kc = rope(kd[:, j * LANES:(j + 1) * LANES], cd, sd)
            kd_ref[rows, (2 * j) * LANES:(2 * j + 1) * LANES] = jnp.where(first_slot, kc, zero).astype(BF16)
            kd_ref[rows, (2 * j + 1) * LANES:(2 * j + 2) * LANES] = jnp.where(first_slot, zero, kc).astype(BF16)
        vd = project(OFF_VD, DIFF_V)
        for j in range(DIFF_HEADS):
            vd_ref[rows, (2 * j) * LANES:(2 * j + 1) * LANES] = vd[:, j * LANES:(j + 1) * LANES].astype(BF16)
            vd_ref[rows, (2 * j + 1) * LANES:(2 * j + 2) * LANES] = one.astype(BF16)

    for chain in range(PROJ_ROW_CHAINS):
        _proj_rows(pl.ds(chain * tm, tm))


def _proj_call(x2d, seq, layer, attn_norm, w_in, qn, kn, ca, sa, cd, sd, tm):
    tokens = x2d.shape[0]
    n_pos = seq // tm
    tok = lambda i: (i, 0)
    pos = lambda i: (i % n_pos, 0)
    resident = functools.partial(pl.BlockSpec, index_map=lambda i: (layer, 0, 0), pipeline_mode=pl.Buffered(1))
    widths = (GQA_Q, 4 * LANES, 2 * LANES, DIFF_QK, 2 * DIFF_QK, 2 * DIFF_V)
    return pl.pallas_call(
        _proj_body,
        grid=(tokens // tm,),
        in_specs=[
            pl.BlockSpec((tm, D_MODEL), tok),
            resident((None, 1, D_MODEL)),
            resident((None, D_MODEL, IN_WIDTH)),
            resident((None, 1, LANES)),
            resident((None, 1, LANES)),
            pl.BlockSpec((tm, LANES), pos),
            pl.BlockSpec((tm, LANES), pos),
            pl.BlockSpec((tm, LANES), pos),
            pl.BlockSpec((tm, LANES), pos),
        ],
        out_specs=[pl.BlockSpec((tm, w), tok) for w in widths],
        out_shape=[jax.ShapeDtypeStruct((tokens, w), BF16) for w in widths],
        compiler_params=pltpu.CompilerParams(
            dimension_semantics=("parallel",), vmem_limit_bytes=VMEM_LIMIT_BYTES),
        name="proj",
    )(x2d, attn_norm, w_in, qn, kn, ca, sa, cd, sd)


def _softmax_pv(s, v):
    m = jnp.max(s, axis=-1, keepdims=True)
    p = jnp.exp2(s - m).astype(BF16)
    return _dot(p, v)


UNROLLED_SCORE_COLUMNS = 8192


def _pipelined_rows(n_blocks, seq, scores, consume):
    per_trip = max(2, UNROLLED_SCORE_COLUMNS // seq)
    assert per_trip % 2 == 0 and n_blocks >= 2
    scores(0, 0)
    looped = ((n_blocks - 1) // per_trip) * per_trip

    def trip(t, carry):
        for u in range(per_trip):
            i = t * per_trip + u
            scores(i + 1, (u + 1) % 2)
            consume(i, u % 2)
        return carry

    lax.fori_loop(0, looped // per_trip, trip, 0)
    for i in range(looped, n_blocks):
        if i + 1 < n_blocks:
            scores(i + 1, (i + 1) % 2)
        consume(i, i % 2)


def _gqa_body(rows, q_ref, k_ref, v_ref, o_ref, s_scr0, s_scr1):
    s_scr = (s_scr0, s_scr1)
    seq = q_ref.shape[0]
    low_half = lax.broadcasted_iota(jnp.int32, (rows, LANES), 1) < HEAD_DIM

    def scores(i, slot):
        q = q_ref[pl.ds(pl.multiple_of(i * rows, rows), rows), :]
        q2 = jnp.concatenate([q[:, :LANES], q[:, LANES:]], axis=0)
        s_scr[slot][0] = _dot_nt(q2, k_ref[:, :LANES])
        s_scr[slot][1] = _dot_nt(q2, k_ref[:, LANES:])

    def consume(i, slot):
        v = v_ref[...]
        r_lo = _softmax_pv(s_scr[slot][0], v)
        r_hi = _softmax_pv(s_scr[slot][1], v)
        cols = []
        for col in range(2):
            lo = r_lo[col * rows:(col + 1) * rows]
            hi = r_hi[col * rows:(col + 1) * rows]
            lo_sw = pltpu.roll(lo, HEAD_DIM, 1)
            hi_sw = pltpu.roll(hi, HEAD_DIM, 1)
            cols.append(jnp.where(low_half, lo / lo_sw, hi_sw / hi))
        out = jnp.concatenate(cols, axis=1)
        o_ref[pl.ds(pl.multiple_of(i * rows, rows), rows), :] = out.astype(o_ref.dtype)

    _pipelined_rows(seq // rows, seq, scores, consume)


def _gqa_call(qg, kg, vg, rows):
    batch, seq, _ = qg.shape
    return pl.pallas_call(
        functools.partial(_gqa_body, rows),
        grid=(batch, GQA_KV_HEADS),
        in_specs=[
            pl.BlockSpec((None, seq, 2 * LANES), lambda b, g: (b, 0, g)),
            pl.BlockSpec((None, seq, 2 * LANES), lambda b, g: (b, 0, g)),
            pl.BlockSpec((None, seq, LANES), lambda b, g: (b, 0, g)),
        ],
        out_specs=pl.BlockSpec((None, seq, 2 * LANES), lambda b, g: (b, 0, g)),
        out_shape=jax.ShapeDtypeStruct((batch, seq, GQA_Q), BF16),
        scratch_shapes=[pltpu.VMEM((2, 2 * rows, seq), F32)] * 2,
        compiler_params=pltpu.CompilerParams(
            dimension_semantics=("parallel", "parallel"),
            vmem_limit_bytes=VMEM_LIMIT_BYTES),
        name="gqa_attn",
    )(qg, kg, vg)


def _diff_body(lambda_init, rows, q_ref, k_ref, v_ref, lq1_ref, lk1_ref, lq2_ref, lk2_ref, sn_ref, o_ref,
               s_scr0, s_scr1):
    s_scr = (s_scr0, s_scr1)
    seq = q_ref.shape[0]
    lam = (jnp.exp(jnp.sum(lq1_ref[...] * lk1_ref[...], axis=-1, keepdims=True))
           - jnp.exp(jnp.sum(lq2_ref[...] * lk2_ref[...], axis=-1, keepdims=True)) + lambda_init)

    def scores(i, slot):
        q = q_ref[pl.ds(pl.multiple_of(i * rows, rows), rows), :]
        s_scr[slot][0] = _dot_nt(q, k_ref[:, :LANES])
        s_scr[slot][1] = _dot_nt(q, k_ref[:, LANES:])

    def consume(i, slot):
        v = v_ref[...]
        r1 = _softmax_pv(s_scr[slot][0], v)
        r2 = _softmax_pv(s_scr[slot][1], v)
        o = r1[:, :LANES] / r1[:, LANES:] - lam * (r2[:, :LANES] / r2[:, LANES:])
        o = _rms(o, sn_ref[...], DIFF_NORM_EPS) * (1.0 - lambda_init)
        o_ref[pl.ds(pl.multiple_of(i * rows, rows), rows), :] = o.astype(o_ref.dtype)

    _pipelined_rows(seq // rows, seq, scores, consume)


def _diff_call(qd, kd, vd, layer, lq1, lk1, lq2, lk2, sub_norm, lambda_init, rows):
    batch, seq, _ = qd.shape
    const = lambda b, h: (layer, 0, 0)
    return pl.pallas_call(
        functools.partial(_diff_body, lambda_init, rows),
        grid=(batch, DIFF_HEADS),
        in_specs=[
            pl.BlockSpec((None, seq, LANES), lambda b, h: (b, 0, h)),
            pl.BlockSpec((None, seq, 2 * LANES), lambda b, h: (b, 0, h)),
            pl.BlockSpec((None, seq, 2 * LANES), lambda b, h: (b, 0, h)),
            pl.BlockSpec((None, 1, HEAD_DIM), const),
            pl.BlockSpec((None, 1, HEAD_DIM), const),
            pl.BlockSpec((None, 1, HEAD_DIM), const),
            pl.BlockSpec((None, 1, HEAD_DIM), const),
            pl.BlockSpec((None, 1, DIFF_V_DIM), const),
        ],
        out_specs=pl.BlockSpec((None, seq, LANES), lambda b, h: (b, 0, h)),
        out_shape=jax.ShapeDtypeStruct((batch, seq, DIFF_V), BF16),
        scratch_shapes=[pltpu.VMEM((2, rows, seq), F32)] * 2,
        compiler_params=pltpu.CompilerParams(
            dimension_semantics=("parallel", "parallel"),
            vmem_limit_bytes=VMEM_LIMIT_BYTES),
        name="diff_attn",
    )(qd, kd, vd, lq1, lk1, lq2, lk2, sub_norm)


MXU_TILE = 256
FF_SPLITS = (0, 6 * MXU_TILE, D_FF)


def _mlp_body(apply_final_norm, x_ref, og_ref, od_ref, wo_ref, fn_ref, wgu_ref, wd_ref, final_ref, y_ref):
    x = x_ref[...] + (_dot(og_ref[...], wo_ref[:GQA_Q, :]) + _dot(od_ref[...], wo_ref[GQA_Q:, :]))
    h = _rms(x, fn_ref[...], NORM_EPS).astype(BF16)
    ffn = None
    for c0, c1 in zip(FF_SPLITS[:-1], FF_SPLITS[1:]):
        gate = _dot(h, wgu_ref[:, c0:c1])
        up = _dot(h, wgu_ref[:, D_FF + c0:D_FF + c1])
        act = (gate * jax.nn.sigmoid(gate) * up).astype(BF16)
        down = _dot(act, wd_ref[c0:c1, :])
        ffn = down if ffn is None else ffn + down
    x = x + ffn
    if apply_final_norm:
        x = _rms(x, final_ref[...], NORM_EPS)
    y_ref[...] = x


def _mlp_call(x2d, og, od, layer, w_out, ffn_norm, w_gate_up, w_down, final_norm, apply_final_norm, tm):
    tokens = x2d.shape[0]
    tok = lambda i: (i, 0)
    resident = functools.partial(pl.BlockSpec, index_map=lambda i: (layer, 0, 0), pipeline_mode=pl.Buffered(1))
    return pl.pallas_call(
        functools.partial(_mlp_body, apply_final_norm),
        grid=(tokens // tm,),
        in_specs=[
            pl.BlockSpec((tm, D_MODEL), tok),
            pl.BlockSpec((tm, GQA_Q), tok),
            pl.BlockSpec((tm, DIFF_V), tok),
            resident((None, GQA_Q + DIFF_V, D_MODEL)),
            resident((None, 1, D_MODEL)),
            resident((None, D_MODEL, 2 * D_FF)),
            resident((None, D_FF, D_MODEL)),
            pl.BlockSpec((1, D_MODEL), lambda i: (0, 0), pipeline_mode=pl.Buffered(1)),
        ],
        out_specs=pl.BlockSpec((tm, D_MODEL), tok),
        out_shape=jax.ShapeDtypeStruct((tokens, D_MODEL), F32),
        compiler_params=pltpu.CompilerParams(
            dimension_semantics=("parallel",), vmem_limit_bytes=VMEM_LIMIT_BYTES),
        name="mlp",
    )(x2d, og, od, w_out, ffn_norm, w_gate_up, w_down, final_norm)


def _permute_gqa_columns(a):
    lead = a.shape[:-1]
    a = a.reshape(lead + (a.shape[-1] // LANES, 2, 2, 2, AXIAL_DIM // 2))
    n = len(lead)
    a = a.transpose(tuple(range(n)) + (n, n + 3, n + 1, n + 2, n + 4))
    return a.reshape(lead + (-1,))


def _permute_diff_columns(a):
    lead = a.shape[:-1]
    a = a.reshape(lead + (a.shape[-1] // LANES, 2, 2, HEAD_DIM // 2))
    n = len(lead)
    a = a.transpose(tuple(range(n)) + (n, n + 2, n + 1, n + 3))
    return a.reshape(lead + (-1,))


def _permute_projection_columns(w_in):
    return jnp.concatenate([
        _permute_gqa_columns(w_in[..., OFF_QG:OFF_VG]), w_in[..., OFF_VG:OFF_QD],
        _permute_diff_columns(w_in[..., OFF_QD:OFF_VD]), w_in[..., OFF_VD:]], axis=-1)


def _rope_tables(seq):
    def inv_freq(dim):
        return ROPE_THETA ** (-jnp.arange(0, dim, 2, dtype=F32) / dim)

    t = lax.broadcasted_iota(jnp.int32, (seq, LANES), 0)
    lane = lax.broadcasted_iota(jnp.int32, (seq, LANES), 1)
    sign = jnp.where(lane < HEAD_DIM, -1.0, 1.0).astype(F32)
    pos_a = jnp.where((lane % AXIAL_DIM) < AXIAL_DIM // 2, t // GRID_W, t % GRID_W)
    ang_a = pos_a.astype(F32) * jnp.tile(inv_freq(AXIAL_DIM), LANES // (AXIAL_DIM // 2))[None, :]
    ang_d = t.astype(F32) * jnp.tile(inv_freq(HEAD_DIM), LANES // (HEAD_DIM // 2))[None, :]
    return jnp.cos(ang_a), jnp.sin(ang_a) * sign, jnp.cos(ang_d), jnp.sin(ang_d) * sign


def _trunk(x, params, proj_tm, mlp_tm, gqa_rows, diff_rows):
    batch, seq, _ = x.shape
    tokens = batch * seq
    (w_in, w_out, attn_norm, qn, kn, lq1, lk1, lq2, lk2, sub_norm, ffn_norm, w_gate_up, w_down,
     final_norm) = params
    depth = w_in.shape[0]
    tables = _rope_tables(seq)
    x2d = x.reshape(tokens, D_MODEL)
    for l in range(depth):
        lambda_init = 0.8 - 0.6 * math.exp(-0.3 * l)
        qg, kg, vg, qd, kd, vd = _proj_call(x2d, seq, l, attn_norm, w_in, qn, kn, *tables, proj_tm)
        shape3 = lambda a: a.reshape(batch, seq, a.shape[-1])
        og = _gqa_call(shape3(qg), shape3(kg), shape3(vg), gqa_rows)
        od = _diff_call(shape3(qd), shape3(kd), shape3(vd), l, lq1, lk1, lq2, lk2, sub_norm, lambda_init,
                        diff_rows)
        x2d = _mlp_call(x2d, og.reshape(tokens, GQA_Q), od.reshape(tokens, DIFF_V), l, w_out, ffn_norm,
                        w_gate_up, w_down, final_norm, l == depth - 1, mlp_tm)
    return x2d.reshape(batch, seq, D_MODEL)


def _prepare_params(w_in, w_out, attn_norm, gqa_q_norm, gqa_k_norm, diff_lambda_q1, diff_lambda_k1,
                    diff_lambda_q2, diff_lambda_k2, diff_sub_norm, ffn_norm, w_gate_up, w_down, final_norm):
    depth = w_in.shape[0]
    row = lambda a: a.reshape(depth, 1, a.shape[-1])
    tile2 = lambda a: _permute_gqa_columns(jnp.tile(a, (1, LANES // HEAD_DIM))).reshape(depth, 1, LANES)
    return (
        _permute_projection_columns(w_in).astype(BF16), w_out.astype(BF16), row(attn_norm),
        tile2(gqa_q_norm), tile2(gqa_k_norm),
        row(diff_lambda_q1), row(diff_lambda_k1), row(diff_lambda_q2), row(diff_lambda_k2),
        row(diff_sub_norm), row(ffn_norm), w_gate_up.astype(BF16), w_down.astype(BF16),
        final_norm.reshape(1, D_MODEL),
    )


def kernel(x_prompt, x_sample, w_in, w_out, attn_norm, gqa_q_norm, gqa_k_norm, diff_lambda_q1, diff_lambda_k1, diff_lambda_q2, diff_lambda_k2, diff_sub_norm, ffn_norm, w_gate_up, w_down, final_norm):
    params = _prepare_params(w_in, w_out, attn_norm, gqa_q_norm, gqa_k_norm, diff_lambda_q1, diff_lambda_k1,
                             diff_lambda_q2, diff_lambda_k2, diff_sub_norm, ffn_norm, w_gate_up, w_down,
                             final_norm)
    tiles = dict(proj_tm=1024, mlp_tm=512, gqa_rows=128, diff_rows=256)
    y_prompt = _trunk(x_prompt, params, **tiles)
    y_sample = _trunk(x_sample, params, **tiles)
    return (y_prompt, y_sample)
```

```python
import functools
import math

import jax
import jax.numpy as jnp
from jax import lax
from jax.experimental import pallas as pl
from jax.experimental.pallas import tpu as pltpu

D_MODEL = 1024
GRID_W = 64
HEAD_DIM = 64
AXIAL_DIM = HEAD_DIM // 2
GQA_HEADS = 8
GQA_KV_HEADS = 2
DIFF_HEADS = 4
DIFF_V_DIM = 2 * HEAD_DIM
GQA_Q = GQA_HEADS * HEAD_DIM
GQA_KV = GQA_KV_HEADS * HEAD_DIM
DIFF_QK = DIFF_HEADS * 2 * HEAD_DIM
DIFF_V = DIFF_HEADS * DIFF_V_DIM
IN_WIDTH = GQA_Q + 2 * GQA_KV + 2 * DIFF_QK + DIFF_V
D_FF = 2816
ROPE_THETA = 10000.0
NORM_EPS = 1e-6
DIFF_NORM_EPS = 1e-5

LANES = 128
PROJ_ROW_CHAINS = 2
VMEM_LIMIT_BYTES = 56 * 1024 * 1024
Q_PRESCALE = (HEAD_DIM ** -0.5) * math.log2(math.e)

OFF_QG = 0
OFF_KG = GQA_Q
OFF_VG = OFF_KG + GQA_KV
OFF_QD = OFF_VG + GQA_KV
OFF_KD = OFF_QD + DIFF_QK
OFF_VD = OFF_KD + DIFF_QK

BF16 = jnp.bfloat16
F32 = jnp.float32


def _dot(a, b):
    return jnp.dot(a, b, preferred_element_type=F32)


def _dot_nt(a, b):
    return lax.dot_general(a, b, (((1,), (1,)), ((), ())), preferred_element_type=F32)


def _rms(x, gain, eps):
    ms = jnp.mean(x * x, axis=-1, keepdims=True)
    return x * lax.rsqrt(ms + eps) * gain


def _proj_body(x_ref, an_ref, w_ref, qn_ref, kn_ref, ca_ref, sa_ref, cd_ref, sd_ref,
               qg_ref, kg_ref, vg_ref, qd_ref, kd_ref, vd_ref):
    tm = x_ref.shape[0] // PROJ_ROW_CHAINS
    lane = lax.broadcasted_iota(jnp.int32, (tm, LANES), 1)
    low_half = lane < HEAD_DIM
    first_slot = (lane % HEAD_DIM) < (HEAD_DIM // 2)
    r = (lax.broadcasted_iota(jnp.int32, (2 * LANES, LANES), 0) % HEAD_DIM) // (HEAD_DIM // 2)
    c = (lax.broadcasted_iota(jnp.int32, (2 * LANES, LANES), 1) % HEAD_DIM) // (HEAD_DIM // 2)
    blk = (r == c).astype(BF16)

    def head_norm(xc, gain):
        y = xc * xc
        hi = y.astype(BF16)
        lo = (y - hi.astype(F32)).astype(BF16)
        ss = _dot(jnp.concatenate([hi, lo], axis=1), blk)
        return xc * lax.rsqrt(ss * (1.0 / HEAD_DIM) + NORM_EPS) * gain

    def rope(xc, cos, sin_signed):
        return xc * cos + pltpu.roll(xc, HEAD_DIM, 1) * sin_signed

    zero = jnp.zeros((tm, LANES), F32)
    one = jnp.ones((tm, LANES), F32)

    def _proj_rows(rows):
        h = _rms(x_ref[rows, :], an_ref[...], NORM_EPS).astype(BF16)
        proj = _dot(h, w_ref[...])
        ca, sa, cd, sd = ca_ref[rows, :], sa_ref[rows, :], cd_ref[rows, :], sd_ref[rows, :]

        def project(offset, width):
            return proj[:, offset:offset + width]

        qg = project(OFF_QG, GQA_Q)
        for j in range(GQA_Q // LANES):
            xc = rope(head_norm(qg[:, j * LANES:(j + 1) * LANES], qn_ref[...]), ca, sa) * Q_PRESCALE
            qg_ref[rows, j * LANES:(j + 1) * LANES] = xc.astype(BF16)

        kv = project(OFF_KG, 2 * GQA_KV)
        kc = rope(head_norm(kv[:, :LANES], kn_ref[...]), ca, sa)
        to_first = pltpu.roll(kc, LANES - HEAD_DIM // 2, 1)
        to_second = pltpu.roll(kc, HEAD_DIM // 2, 1)
        kg_ref[rows, 0 * LANES:1 * LANES] = jnp.where(first_slot, kc, zero).astype(BF16)
        kg_ref[rows, 1 * LANES:2 * LANES] = jnp.where(first_slot, zero, to_second).astype(BF16)
        kg_ref[rows, 2 * LANES:3 * LANES] = jnp.where(first_slot, to_first, zero).astype(BF16)
        kg_ref[rows, 3 * LANES:4 * LANES] = jnp.where(first_slot, zero, kc).astype(BF16)

        vc = kv[:, LANES:]
        vg_ref[rows, 0 * LANES:1 * LANES] = jnp.where(low_half, vc, one).astype(BF16)
        vg_ref[rows, 1 * LANES:2 * LANES] = jnp.where(low_half, pltpu.roll(vc, HEAD_DIM, 1), one).astype(BF16)

        qd = project(OFF_QD, DIFF_QK)
        for j in range(DIFF_HEADS):
            xc = rope(qd[:, j * LANES:(j + 1) * LANES], cd, sd) * Q_PRESCALE
            qd_ref[rows, j * LANES:(j + 1) * LANES] = xc.astype(BF16)
        kd = project(OFF_KD, DIFF_QK)
        for j in range(DIFF_HEADS):
            kc = rope(kd[:, j * LANES:(j + 1) * LANES], cd, sd)
            kd_ref[rows, (2 * j) * LANES:(2 * j + 1) * LANES] = jnp.where(first_slot, kc, zero).astype(BF16)
            kd_ref[rows, (2 * j + 1) * LANES:(2 * j + 2) * LANES] = jnp.where(first_slot, zero, kc).astype(BF16)
        vd = project(OFF_VD, DIFF_V)
        for j in range(DIFF_HEADS):
            vd_ref[rows, (2 * j) * LANES:(2 * j + 1) * LANES] = vd[:, j * LANES:(j + 1) * LANES].astype(BF16)
            vd_ref[rows, (2 * j + 1) * LANES:(2 * j + 2) * LANES] = one.astype(BF16)

    for chain in range(PROJ_ROW_CHAINS):
        _proj_rows(pl.ds(chain * tm, tm))


def _proj_call(x2d, seq, layer, attn_norm, w_in, qn, kn, ca, sa, cd, sd, tm):
    tokens = x2d.shape[0]
    n_pos = seq // tm
    tok = lambda i: (i, 0)
    pos = lambda i: (i % n_pos, 0)
    resident = functools.partial(pl.BlockSpec, index_map=lambda i: (layer, 0, 0), pipeline_mode=pl.Buffered(1))
    widths = (GQA_Q, 4 * LANES, 2 * LANES, DIFF_QK, 2 * DIFF_QK, 2 * DIFF_V)
    return pl.pallas_call(
        _proj_body,
        grid=(tokens // tm,),
        in_specs=[
            pl.BlockSpec((tm, D_MODEL), tok),
            resident((None, 1, D_MODEL)),
            resident((None, D_MODEL, IN_WIDTH)),
            resident((None, 1, LANES)),
            resident((None, 1, LANES)),
            pl.BlockSpec((tm, LANES), pos),
            pl.BlockSpec((tm, LANES), pos),
            pl.BlockSpec((tm, LANES), pos),
            pl.BlockSpec((tm, LANES), pos),
        ],
        out_specs=[pl.BlockSpec((tm, w), tok) for w in widths],
        out_shape=[jax.ShapeDtypeStruct((tokens, w), BF16) for w in widths],
        compiler_params=pltpu.CompilerParams(
            dimension_semantics=("parallel",), vmem_limit_bytes=VMEM_LIMIT_BYTES),
        name="proj",
    )(x2d, attn_norm, w_in, qn, kn, ca, sa, cd, sd)


def _softmax_pv(s, v):
    m = jnp.max(s, axis=-1, keepdims=True)
    p = jnp.exp2(s - m).astype(BF16)
    return _dot(p, v)


PIPELINE_BLOCKS_PER_TRIP = 4


def _pipelined_rows(n_blocks, scores, consume):
    per_trip = PIPELINE_BLOCKS_PER_TRIP
    assert per_trip % 2 == 0 and n_blocks >= 2
    scores(0, 0)
    looped = ((n_blocks - 1) // per_trip) * per_trip

    def trip(t, carry):
        for u in range(per_trip):
            i = t * per_trip + u
            scores(i + 1, (u + 1) % 2)
            consume(i, u % 2)
        return carry

    lax.fori_loop(0, looped // per_trip, trip, 0)
    for i in range(looped, n_blocks):
        if i + 1 < n_blocks:
            scores(i + 1, (i + 1) % 2)
        consume(i, i % 2)


def _gqa_body(rows, q_ref, k_ref, v_ref, o_ref, s_scr0, s_scr1):
    s_scr = (s_scr0, s_scr1)
    seq = q_ref.shape[0]
    low_half = lax.broadcasted_iota(jnp.int32, (rows, LANES), 1) < HEAD_DIM

    def scores(i, slot):
        q = q_ref[pl.ds(pl.multiple_of(i * rows, rows), rows), :]
        q2 = jnp.concatenate([q[:, :LANES], q[:, LANES:]], axis=0)
        s_scr[slot][0] = _dot_nt(q2, k_ref[:, :LANES])
        s_scr[slot][1] = _dot_nt(q2, k_ref[:, LANES:])

    def consume(i, slot):
        v = v_ref[...]
        r_lo = _softmax_pv(s_scr[slot][0], v)
        r_hi = _softmax_pv(s_scr[slot][1], v)
        cols = []
        for col in range(2):
            lo = r_lo[col * rows:(col + 1) * rows]
            hi = r_hi[col * rows:(col + 1) * rows]
            lo_sw = pltpu.roll(lo, HEAD_DIM, 1)
            hi_sw = pltpu.roll(hi, HEAD_DIM, 1)
            cols.append(jnp.where(low_half, lo / lo_sw, hi_sw / hi))
        out = jnp.concatenate(cols, axis=1)
        o_ref[pl.ds(pl.multiple_of(i * rows, rows), rows), :] = out.astype(o_ref.dtype)

    _pipelined_rows(seq // rows, scores, consume)


def _gqa_call(qg, kg, vg, rows):
    batch, seq, _ = qg.shape
    return pl.pallas_call(
        functools.partial(_gqa_body, rows),
        grid=(batch, GQA_KV_HEADS),
        in_specs=[
            pl.BlockSpec((None, seq, 2 * LANES), lambda b, g: (b, 0, g)),
            pl.BlockSpec((None, seq, 2 * LANES), lambda b, g: (b, 0, g)),
            pl.BlockSpec((None, seq, LANES), lambda b, g: (b, 0, g)),
        ],
        out_specs=pl.BlockSpec((None, seq, 2 * LANES), lambda b, g: (b, 0, g)),
        out_shape=jax.ShapeDtypeStruct((batch, seq, GQA_Q), BF16),
        scratch_shapes=[pltpu.VMEM((2, 2 * rows, seq), F32)] * 2,
        compiler_params=pltpu.CompilerParams(
            dimension_semantics=("parallel", "parallel"),
            vmem_limit_bytes=VMEM_LIMIT_BYTES),
        name="gqa_attn",
    )(qg, kg, vg)


def _diff_body(lambda_init, rows, q_ref, k_ref, v_ref, lq1_ref, lk1_ref, lq2_ref, lk2_ref, sn_ref, o_ref,
               s_scr0, s_scr1):
    s_scr = (s_scr0, s_scr1)
    seq = q_ref.shape[0]
    lam = (jnp.exp(jnp.sum(lq1_ref[...] * lk1_ref[...], axis=-1, keepdims=True))
           - jnp.exp(jnp.sum(lq2_ref[...] * lk2_ref[...], axis=-1, keepdims=True)) + lambda_init)

    def scores(i, slot):
        q = q_ref[pl.ds(pl.multiple_of(i * rows, rows), rows), :]
        s_scr[slot][0] = _dot_nt(q, k_ref[:, :LANES])
        s_scr[slot][1] = _dot_nt(q, k_ref[:, LANES:])

    def consume(i, slot):
        v = v_ref[...]
        r1 = _softmax_pv(s_scr[slot][0], v)
        r2 = _softmax_pv(s_scr[slot][1], v)
        o = r1[:, :LANES] / r1[:, LANES:] - lam * (r2[:, :LANES] / r2[:, LANES:])
        o = _rms(o, sn_ref[...], DIFF_NORM_EPS) * (1.0 - lambda_init)
        o_ref[pl.ds(pl.multiple_of(i * rows, rows), rows), :] = o.astype(o_ref.dtype)

    _pipelined_rows(seq // rows, scores, consume)


def _diff_call(qd, kd, vd, layer, lq1, lk1, lq2, lk2, sub_norm, lambda_init, rows):
    batch, seq, _ = qd.shape
    const = lambda b, h: (layer, 0, 0)
    return pl.pallas_call(
        functools.partial(_diff_body, lambda_init, rows),
        grid=(batch, DIFF_HEADS),
        in_specs=[
            pl.BlockSpec((None, seq, LANES), lambda b, h: (b, 0, h)),
            pl.BlockSpec((None, seq, 2 * LANES), lambda b, h: (b, 0, h)),
            pl.BlockSpec((None, seq, 2 * LANES), lambda b, h: (b, 0, h)),
            pl.BlockSpec((None, 1, HEAD_DIM), const),
            pl.BlockSpec((None, 1, HEAD_DIM), const),
            pl.BlockSpec((None, 1, HEAD_DIM), const),
            pl.BlockSpec((None, 1, HEAD_DIM), const),
            pl.BlockSpec((None, 1, DIFF_V_DIM), const),
        ],
        out_specs=pl.BlockSpec((None, seq, LANES), lambda b, h: (b, 0, h)),
        out_shape=jax.ShapeDtypeStruct((batch, seq, DIFF_V), BF16),
        scratch_shapes=[pltpu.VMEM((2, rows, seq), F32)] * 2,
        compiler_params=pltpu.CompilerParams(
            dimension_semantics=("parallel", "parallel"),
            vmem_limit_bytes=VMEM_LIMIT_BYTES),
        name="diff_attn",
    )(qd, kd, vd, lq1, lk1, lq2, lk2, sub_norm)


MXU_TILE = 256
FF_SPLITS = (0, 6 * MXU_TILE, D_FF)


def _mlp_body(apply_final_norm, x_ref, og_ref, od_ref, wo_ref, fn_ref, wgu_ref, wd_ref, final_ref, y_ref):
    x = x_ref[...] + (_dot(og_ref[...], wo_ref[:GQA_Q, :]) + _dot(od_ref[...], wo_ref[GQA_Q:, :]))
    h = _rms(x, fn_ref[...], NORM_EPS).astype(BF16)
    ffn = None
    for c0, c1 in zip(FF_SPLITS[:-1], FF_SPLITS[1:]):
        gate = _dot(h, wgu_ref[:, c0:c1])
        up = _dot(h, wgu_ref[:, D_FF + c0:D_FF + c1])
        act = (gate * jax.nn.sigmoid(gate) * up).astype(BF16)
        down = _dot(act, wd_ref[c0:c1, :])
        ffn = down if ffn is None else ffn + down
    x = x + ffn
    if apply_final_norm:
        x = _rms(x, final_ref[...], NORM_EPS)
    y_ref[...] = x


def _mlp_call(x2d, og, od, layer, w_out, ffn_norm, w_gate_up, w_down, final_norm, apply_final_norm, tm):
    tokens = x2d.shape[0]
    tok = lambda i: (i, 0)
    resident = functools.partial(pl.BlockSpec, index_map=lambda i: (layer, 0, 0), pipeline_mode=pl.Buffered(1))
    return pl.pallas_call(
        functools.partial(_mlp_body, apply_final_norm),
        grid=(tokens // tm,),
        in_specs=[
            pl.BlockSpec((tm, D_MODEL), tok),
            pl.BlockSpec((tm, GQA_Q), tok),
            pl.BlockSpec((tm, DIFF_V), tok),
            resident((None, GQA_Q + DIFF_V, D_MODEL)),
            resident((None, 1, D_MODEL)),
            resident((None, D_MODEL, 2 * D_FF)),
            resident((None, D_FF, D_MODEL)),
            pl.BlockSpec((1, D_MODEL), lambda i: (0, 0), pipeline_mode=pl.Buffered(1)),
        ],
        out_specs=pl.BlockSpec((tm, D_MODEL), tok),
        out_shape=jax.ShapeDtypeStruct((tokens, D_MODEL), F32),
        compiler_params=pltpu.CompilerParams(
            dimension_semantics=("parallel",), vmem_limit_bytes=VMEM_LIMIT_BYTES),
        name="mlp",
    )(x2d, og, od, w_out, ffn_norm, w_gate_up, w_down, final_norm)


def _permute_gqa_columns(a):
    lead = a.shape[:-1]
    a = a.reshape(lead + (a.shape[-1] // LANES, 2, 2, 2, AXIAL_DIM // 2))
    n = len(lead)
    a = a.transpose(tuple(range(n)) + (n, n + 3, n + 1, n + 2, n + 4))
    return a.reshape(lead + (-1,))


def _permute_diff_columns(a):
    lead = a.shape[:-1]
    a = a.reshape(lead + (a.shape[-1] // LANES, 2, 2, HEAD_DIM // 2))
    n = len(lead)
    a = a.transpose(tuple(range(n)) + (n, n + 2, n + 1, n + 3))
    return a.reshape(lead + (-1,))


def _permute_projection_columns(w_in):
    return jnp.concatenate([
        _permute_gqa_columns(w_in[..., OFF_QG:OFF_VG]), w_in[..., OFF_VG:OFF_QD],
        _permute_diff_columns(w_in[..., OFF_QD:OFF_VD]), w_in[..., OFF_VD:]], axis=-1)


def _rope_tables(seq):
    def inv_freq(dim):
        return ROPE_THETA ** (-jnp.arange(0, dim, 2, dtype=F32) / dim)

    t = lax.broadcasted_iota(jnp.int32, (seq, LANES), 0)
    lane = lax.broadcasted_iota(jnp.int32, (seq, LANES), 1)
    sign = jnp.where(lane < HEAD_DIM, -1.0, 1.0).astype(F32)
    pos_a = jnp.where((lane % AXIAL_DIM) < AXIAL_DIM // 2, t // GRID_W, t % GRID_W)
    ang_a = pos_a.astype(F32) * jnp.tile(inv_freq(AXIAL_DIM), LANES // (AXIAL_DIM // 2))[None, :]
    ang_d = t.astype(F32) * jnp.tile(inv_freq(HEAD_DIM), LANES // (HEAD_DIM // 2))[None, :]
    return jnp.cos(ang_a), jnp.sin(ang_a) * sign, jnp.cos(ang_d), jnp.sin(ang_d) * sign


def _trunk(x, params, proj_tm, mlp_tm, gqa_rows, diff_rows):
    batch, seq, _ = x.shape
    tokens = batch * seq
    (w_in, w_out, attn_norm, qn, kn, lq1, lk1, lq2, lk2, sub_norm, ffn_norm, w_gate_up, w_down,
     final_norm) = params
    depth = w_in.shape[0]
    tables = _rope_tables(seq)
    x2d = x.reshape(tokens, D_MODEL)
    for l in range(depth):
        lambda_init = 0.8 - 0.6 * math.exp(-0.3 * l)
        qg, kg, vg, qd, kd, vd = _proj_call(x2d, seq, l, attn_norm, w_in, qn, kn, *tables, proj_tm)
        shape3 = lambda a: a.reshape(batch, seq, a.shape[-1])
        og = _gqa_call(shape3(qg), shape3(kg), shape3(vg), gqa_rows)
        od = _diff_call(shape3(qd), shape3(kd), shape3(vd), l, lq1, lk1, lq2, lk2, sub_norm, lambda_init,
                        diff_rows)
        x2d = _mlp_call(x2d, og.reshape(tokens, GQA_Q), od.reshape(tokens, DIFF_V), l, w_out, ffn_norm,
                        w_gate_up, w_down, final_norm, l == depth - 1, mlp_tm)
    return x2d.reshape(batch, seq, D_MODEL)


TILES = dict(proj_tm=1024, mlp_tm=512, gqa_rows=128, diff_rows=256)


def _prepare_params(w_in, w_out, attn_norm, gqa_q_norm, gqa_k_norm, diff_lambda_q1, diff_lambda_k1,
                    diff_lambda_q2, diff_lambda_k2, diff_sub_norm, ffn_norm, w_gate_up, w_down, final_norm):
    depth = w_in.shape[0]
    row = lambda a: a.reshape(depth, 1, a.shape[-1])
    tile2 = lambda a: _permute_gqa_columns(jnp.tile(a, (1, LANES // HEAD_DIM))).reshape(depth, 1, LANES)
    return (
        _permute_projection_columns(w_in).astype(BF16), w_out.astype(BF16), row(attn_norm),
        tile2(gqa_q_norm), tile2(gqa_k_norm),
        row(diff_lambda_q1), row(diff_lambda_k1), row(diff_lambda_q2), row(diff_lambda_k2),
        row(diff_sub_norm), row(ffn_norm), w_gate_up.astype(BF16), w_down.astype(BF16),
        final_norm.reshape(1, D_MODEL),
    )


def kernel(x_prompt, x_sample, w_in, w_out, attn_norm, gqa_q_norm, gqa_k_norm, diff_lambda_q1, diff_lambda_k1, diff_lambda_q2, diff_lambda_k2, diff_sub_norm, ffn_norm, w_gate_up, w_down, final_norm):
    params = _prepare_params(w_in, w_out, attn_norm, gqa_q_norm, gqa_k_norm, diff_lambda_q1, diff_lambda_k1,
                             diff_lambda_q2, diff_lambda_k2, diff_sub_norm, ffn_norm, w_gate_up, w_down,
                             final_norm)
    y_prompt = _trunk(x_prompt, params, **TILES)
    y_sample = _trunk(x_sample, params, **TILES)
    return (y_prompt, y_sample)
```

```python
import functools
import math

import jax
import jax.numpy as jnp
from jax import lax
from jax.experimental import pallas as pl
from jax.experimental.pallas import tpu as pltpu

D_MODEL = 1024
GRID_W = 64
HEAD_DIM = 64
AXIAL_DIM = HEAD_DIM // 2
GQA_HEADS = 8
GQA_KV_HEADS = 2
DIFF_HEADS = 4
DIFF_V_DIM = 2 * HEAD_DIM
GQA_Q = GQA_HEADS * HEAD_DIM
GQA_KV = GQA_KV_HEADS * HEAD_DIM
DIFF_QK = DIFF_HEADS * 2 * HEAD_DIM
DIFF_V = DIFF_HEADS * DIFF_V_DIM
IN_WIDTH = GQA_Q + 2 * GQA_KV + 2 * DIFF_QK + DIFF_V
D_FF = 2816
ROPE_THETA = 10000.0
NORM_EPS = 1e-6
DIFF_NORM_EPS = 1e-5

LANES = 128
PROJ_ROW_CHAINS = 2
VMEM_LIMIT_BYTES = 56 * 1024 * 1024
Q_PRESCALE = (HEAD_DIM ** -0.5) * math.log2(math.e)

OFF_QG = 0
OFF_KG = GQA_Q
OFF_VG = OFF_KG + GQA_KV
OFF_QD = OFF_VG + GQA_KV
OFF_KD = OFF_QD + DIFF_QK
OFF_VD = OFF_KD + DIFF_QK

BF16 = jnp.bfloat16
F32 = jnp.float32


def _dot(a, b):
    return jnp.dot(a, b, preferred_element_type=F32)


def _dot_nt(a, b):
    return lax.dot_general(a, b, (((1,), (1,)), ((), ())), preferred_element_type=F32)


def _rms(x, gain, eps):
    ms = jnp.mean(x * x, axis=-1, keepdims=True)
    return x * lax.rsqrt(ms + eps) * gain


def _proj_body(x_ref, an_ref, w_ref, qn_ref, kn_ref, ca_ref, sa_ref, cd_ref, sd_ref,
               qg_ref, kg_ref, vg_ref, qd_ref, kd_ref, vd_ref):
    tm = x_ref.shape[0] // PROJ_ROW_CHAINS
    lane = lax.broadcasted_iota(jnp.int32, (tm, LANES), 1)
    low_half = lane < HEAD_DIM
    first_slot = (lane % HEAD_DIM) < (HEAD_DIM // 2)
    r = (lax.broadcasted_iota(jnp.int32, (2 * LANES, LANES), 0) % HEAD_DIM) // (HEAD_DIM // 2)
    c = (lax.broadcasted_iota(jnp.int32, (2 * LANES, LANES), 1) % HEAD_DIM) // (HEAD_DIM // 2)
    blk = (r == c).astype(BF16)

    def head_norm(xc, gain):
        y = xc * xc
        hi = y.astype(BF16)
        lo = (y - hi.astype(F32)).astype(BF16)
        ss = _dot(jnp.concatenate([hi, lo], axis=1), blk)
        return xc * lax.rsqrt(ss * (1.0 / HEAD_DIM) + NORM_EPS) * gain

    def rope(xc, cos, sin_signed):
        return xc * cos + pltpu.roll(xc, HEAD_DIM, 1) * sin_signed

    zero = jnp.zeros((tm, LANES), F32)
    one = jnp.ones((tm, LANES), F32)

    def _proj_rows(rows):
        h = _rms(x_ref[rows, :], an_ref[...], NORM_EPS).astype(BF16)
        proj = _dot(h, w_ref[...])
        ca, sa, cd, sd = ca_ref[rows, :], sa_ref[rows, :], cd_ref[rows, :], sd_ref[rows, :]

        def project(offset, width):
            return proj[:, offset:offset + width]

        qg = project(OFF_QG, GQA_Q)
        for j in range(GQA_Q // LANES):
            xc = rope(head_norm(qg[:, j * LANES:(j + 1) * LANES], qn_ref[...]), ca, sa) * Q_PRESCALE
            qg_ref[rows, j * LANES:(j + 1) * LANES] = xc.astype(BF16)

        kv = project(OFF_KG, 2 * GQA_KV)
        kc = rope(head_norm(kv[:, :LANES], kn_ref[...]), ca, sa)
        to_first = pltpu.roll(kc, LANES - HEAD_DIM // 2, 1)
        to_second = pltpu.roll(kc, HEAD_DIM // 2, 1)
        kg_ref[rows, 0 * LANES:1 * LANES] = jnp.where(first_slot, kc, zero).astype(BF16)
        kg_ref[rows, 1 * LANES:2 * LANES] = jnp.where(first_slot, zero, to_second).astype(BF16)
        kg_ref[rows, 2 * LANES:3 * LANES] = jnp.where(first_slot, to_first, zero).astype(BF16)
        kg_ref[rows, 3 * LANES:4 * LANES] = jnp.where(first_slot, zero, kc).astype(BF16)

        vc = kv[:, LANES:]
        vg_ref[rows, 0 * LANES:1 * LANES] = jnp.where(low_half, vc, one).astype(BF16)
        vg_ref[rows, 1 * LANES:2 * LANES] = jnp.where(low_half, pltpu.roll(vc, HEAD_DIM, 1), one).astype(BF16)

        qd = project(OFF_QD, DIFF_QK)
        for j in range(DIFF_HEADS):
            xc = rope(qd[:, j * LANES:(j + 1) * LANES], cd, sd) * Q_PRESCALE
            qd_ref[rows, j * LANES:(j + 1) * LANES] = xc.astype(BF16)
        kd = project(OFF_KD, DIFF_QK)
        for j in range(DIFF_HEADS):
            kc = rope(kd[:, j * LANES:(j + 1) * LANES], cd, sd)
            kd_ref[rows, (2 * j) * LANES:(2 * j + 1) * LANES] = jnp.where(first_slot, kc, zero).astype(BF16)
            kd_ref[rows, (2 * j + 1) * LANES:(2 * j + 2) * LANES] = jnp.where(first_slot, zero, kc).astype(BF16)
        vd = project(OFF_VD, DIFF_V)
        for j in range(DIFF_HEADS):
            vd_ref[rows, (2 * j) * LANES:(2 * j + 1) * LANES] = vd[:, j * LANES:(j + 1) * LANES].astype(BF16)
            vd_ref[rows, (2 * j + 1) * LANES:(2 * j + 2) * LANES] = one.astype(BF16)

    for chain in range(PROJ_ROW_CHAINS):
        _proj_rows(pl.ds(chain * tm, tm))


def _proj_call(x2d, seq, layer, attn_norm, w_in, qn, kn, ca, sa, cd, sd, tm):
    tokens = x2d.shape[0]
    n_pos = seq // tm
    tok = lambda i: (i, 0)
    pos = lambda i: (i % n_pos, 0)
    resident = functools.partial(pl.BlockSpec, index_map=lambda i: (layer, 0, 0), pipeline_mode=pl.Buffered(1))
    widths = (GQA_Q, 4 * LANES, 2 * LANES, DIFF_QK, 2 * DIFF_QK, 2 * DIFF_V)
    return pl.pallas_call(
        _proj_body,
        grid=(tokens // tm,),
        in_specs=[
            pl.BlockSpec((tm, D_MODEL), tok),
            resident((None, 1, D_MODEL)),
            resident((None, D_MODEL, IN_WIDTH)),
            resident((None, 1, LANES)),
            resident((None, 1, LANES)),
            pl.BlockSpec((tm, LANES), pos),
            pl.BlockSpec((tm, LANES), pos),
            pl.BlockSpec((tm, LANES), pos),
            pl.BlockSpec((tm, LANES), pos),
        ],
        out_specs=[pl.BlockSpec((tm, w), tok) for w in widths],
        out_shape=[jax.ShapeDtypeStruct((tokens, w), BF16) for w in widths],
        compiler_params=pltpu.CompilerParams(
            dimension_semantics=("parallel",), vmem_limit_bytes=VMEM_LIMIT_BYTES),
        name="proj",
    )(x2d, attn_norm, w_in, qn, kn, ca, sa, cd, sd)


def _softmax_pv(s, v):
    m = jnp.max(s, axis=-1, keepdims=True)
    p = jnp.exp2(s - m).astype(BF16)
    return _dot(p, v)


PIPELINE_BLOCKS_PER_TRIP = 8


def _pipelined_rows(n_blocks, scores, consume):
    per_trip = PIPELINE_BLOCKS_PER_TRIP
    assert per_trip % 2 == 0 and n_blocks >= 2
    scores(0, 0)
    looped = ((n_blocks - 1) // per_trip) * per_trip

    def trip(t, carry):
        for u in range(per_trip):
            i = t * per_trip + u
            scores(i + 1, (u + 1) % 2)
            consume(i, u % 2)
        return carry

    lax.fori_loop(0, looped // per_trip, trip, 0)
    for i in range(looped, n_blocks):
        if i + 1 < n_blocks:
            scores(i + 1, (i + 1) % 2)
        consume(i, i % 2)


def _gqa_body(rows, q_ref, k_ref, v_ref, o_ref, s_scr0, s_scr1):
    s_scr = (s_scr0, s_scr1)
    seq = q_ref.shape[0]
    low_half = lax.broadcasted_iota(jnp.int32, (rows, LANES), 1) < HEAD_DIM

    def scores(i, slot):
        q = q_ref[pl.ds(pl.multiple_of(i * rows, rows), rows), :]
        q2 = jnp.concatenate([q[:, :LANES], q[:, LANES:]], axis=0)
        s_scr[slot][0] = _dot_nt(q2, k_ref[:, :LANES])
        s_scr[slot][1] = _dot_nt(q2, k_ref[:, LANES:])

    def consume(i, slot):
        v = v_ref[...]
        r_lo = _softmax_pv(s_scr[slot][0], v)
        r_hi = _softmax_pv(s_scr[slot][1], v)
        cols = []
        for col in range(2):
            lo = r_lo[col * rows:(col + 1) * rows]
            hi = r_hi[col * rows:(col + 1) * rows]
            lo_sw = pltpu.roll(lo, HEAD_DIM, 1)
            hi_sw = pltpu.roll(hi, HEAD_DIM, 1)
            cols.append(jnp.where(low_half, lo / lo_sw, hi_sw / hi))
        out = jnp.concatenate(cols, axis=1)
        o_ref[pl.ds(pl.multiple_of(i * rows, rows), rows), :] = out.astype(o_ref.dtype)

    _pipelined_rows(seq // rows, scores, consume)


def _gqa_call(qg, kg, vg, rows):
    batch, seq, _ = qg.shape
    return pl.pallas_call(
        functools.partial(_gqa_body, rows),
        grid=(batch, GQA_KV_HEADS),
        in_specs=[
            pl.BlockSpec((None, seq, 2 * LANES), lambda b, g: (b, 0, g)),
            pl.BlockSpec((None, seq, 2 * LANES), lambda b, g: (b, 0, g)),
            pl.BlockSpec((None, seq, LANES), lambda b, g: (b, 0, g)),
        ],
        out_specs=pl.BlockSpec((None, seq, 2 * LANES), lambda b, g: (b, 0, g)),
        out_shape=jax.ShapeDtypeStruct((batch, seq, GQA_Q), BF16),
        scratch_shapes=[pltpu.VMEM((2, 2 * rows, seq), F32)] * 2,
        compiler_params=pltpu.CompilerParams(
            dimension_semantics=("parallel", "parallel"),
            vmem_limit_bytes=VMEM_LIMIT_BYTES),
        name="gqa_attn",
    )(qg, kg, vg)


def _diff_body(lambda_init, rows, q_ref, k_ref, v_ref, lq1_ref, lk1_ref, lq2_ref, lk2_ref, sn_ref, o_ref,
               s_scr0, s_scr1):
    s_scr = (s_scr0, s_scr1)
    seq = q_ref.shape[0]
    lam = (jnp.exp(jnp.sum(lq1_ref[...] * lk1_ref[...], axis=-1, keepdims=True))
           - jnp.exp(jnp.sum(lq2_ref[...] * lk2_ref[...], axis=-1, keepdims=True)) + lambda_init)

    def scores(i, slot):
        q = q_ref[pl.ds(pl.multiple_of(i * rows, rows), rows), :]
        s_scr[slot][0] = _dot_nt(q, k_ref[:, :LANES])
        s_scr[slot][1] = _dot_nt(q, k_ref[:, LANES:])

    def consume(i, slot):
        v = v_ref[...]
        r1 = _softmax_pv(s_scr[slot][0], v)
        r2 = _softmax_pv(s_scr[slot][1], v)
        o = r1[:, :LANES] / r1[:, LANES:] - lam * (r2[:, :LANES] / r2[:, LANES:])
        o = _rms(o, sn_ref[...], DIFF_NORM_EPS) * (1.0 - lambda_init)
        o_ref[pl.ds(pl.multiple_of(i * rows, rows), rows), :] = o.astype(o_ref.dtype)

    _pipelined_rows(seq // rows, scores, consume)


def _diff_call(qd, kd, vd, layer, lq1, lk1, lq2, lk2, sub_norm, lambda_init, rows):
    batch, seq, _ = qd.shape
    const = lambda b, h: (layer, 0, 0)
    return pl.pallas_call(
        functools.partial(_diff_body, lambda_init, rows),
        grid=(batch, DIFF_HEADS),
        in_specs=[
            pl.BlockSpec((None, seq, LANES), lambda b, h: (b, 0, h)),
            pl.BlockSpec((None, seq, 2 * LANES), lambda b, h: (b, 0, h)),
            pl.BlockSpec((None, seq, 2 * LANES), lambda b, h: (b, 0, h)),
            pl.BlockSpec((None, 1, HEAD_DIM), const),
            pl.BlockSpec((None, 1, HEAD_DIM), const),
            pl.BlockSpec((None, 1, HEAD_DIM), const),
            pl.BlockSpec((None, 1, HEAD_DIM), const),
            pl.BlockSpec((None, 1, DIFF_V_DIM), const),
        ],
        out_specs=pl.BlockSpec((None, seq, LANES), lambda b, h: (b, 0, h)),
        out_shape=jax.ShapeDtypeStruct((batch, seq, DIFF_V), BF16),
        scratch_shapes=[pltpu.VMEM((2, rows, seq), F32)] * 2,
        compiler_params=pltpu.CompilerParams(
            dimension_semantics=("parallel", "parallel"),
            vmem_limit_bytes=VMEM_LIMIT_BYTES),
        name="diff_attn",
    )(qd, kd, vd, lq1, lk1, lq2, lk2, sub_norm)


MXU_TILE = 256
FF_SPLITS = (0, 6 * MXU_TILE, D_FF)


def _mlp_body(apply_final_norm, x_ref, og_ref, od_ref, wo_ref, fn_ref, wgu_ref, wd_ref, final_ref, y_ref):
    x = x_ref[...] + (_dot(og_ref[...], wo_ref[:GQA_Q, :]) + _dot(od_ref[...], wo_ref[GQA_Q:, :]))
    h = _rms(x, fn_ref[...], NORM_EPS).astype(BF16)
    ffn = None
    for c0, c1 in zip(FF_SPLITS[:-1], FF_SPLITS[1:]):
        gate = _dot(h, wgu_ref[:, c0:c1])
        up = _dot(h, wgu_ref[:, D_FF + c0:D_FF + c1])
        act = (gate * jax.nn.sigmoid(gate) * up).astype(BF16)
        down = _dot(act, wd_ref[c0:c1, :])
        ffn = down if ffn is None else ffn + down
    x = x + ffn
    if apply_final_norm:
        x = _rms(x, final_ref[...], NORM_EPS)
    y_ref[...] = x


def _mlp_call(x2d, og, od, layer, w_out, ffn_norm, w_gate_up, w_down, final_norm, apply_final_norm, tm):
    tokens = x2d.shape[0]
    tok = lambda i: (i, 0)
    resident = functools.partial(pl.BlockSpec, index_map=lambda i: (layer, 0, 0), pipeline_mode=pl.Buffered(1))
    return pl.pallas_call(
        functools.partial(_mlp_body, apply_final_norm),
        grid=(tokens // tm,),
        in_specs=[
            pl.BlockSpec((tm, D_MODEL), tok),
            pl.BlockSpec((tm, GQA_Q), tok),
            pl.BlockSpec((tm, DIFF_V), tok),
            resident((None, GQA_Q + DIFF_V, D_MODEL)),
            resident((None, 1, D_MODEL)),
            resident((None, D_MODEL, 2 * D_FF)),
            resident((None, D_FF, D_MODEL)),
            pl.BlockSpec((1, D_MODEL), lambda i: (0, 0), pipeline_mode=pl.Buffered(1)),
        ],
        out_specs=pl.BlockSpec((tm, D_MODEL), tok),
        out_shape=jax.ShapeDtypeStruct((tokens, D_MODEL), F32),
        compiler_params=pltpu.CompilerParams(
            dimension_semantics=("parallel",), vmem_limit_bytes=VMEM_LIMIT_BYTES),
        name="mlp",
    )(x2d, og, od, w_out, ffn_norm, w_gate_up, w_down, final_norm)


def _permute_gqa_columns(a):
    lead = a.shape[:-1]
    a = a.reshape(lead + (a.shape[-1] // LANES, 2, 2, 2, AXIAL_DIM // 2))
    n = len(lead)
    a = a.transpose(tuple(range(n)) + (n, n + 3, n + 1, n + 2, n + 4))
    return a.reshape(lead + (-1,))


def _permute_diff_columns(a):
    lead = a.shape[:-1]
    a = a.reshape(lead + (a.shape[-1] // LANES, 2, 2, HEAD_DIM // 2))
    n = len(lead)
    a = a.transpose(tuple(range(n)) + (n, n + 2, n + 1, n + 3))
    return a.reshape(lead + (-1,))


def _permute_projection_columns(w_in):
    return jnp.concatenate([
        _permute_gqa_columns(w_in[..., OFF_QG:OFF_VG]), w_in[..., OFF_VG:OFF_QD],
        _permute_diff_columns(w_in[..., OFF_QD:OFF_VD]), w_in[..., OFF_VD:]], axis=-1)


def _rope_tables(seq):
    def inv_freq(dim):
        return ROPE_THETA ** (-jnp.arange(0, dim, 2, dtype=F32) / dim)

    t = lax.broadcasted_iota(jnp.int32, (seq, LANES), 0)
    lane = lax.broadcasted_iota(jnp.int32, (seq, LANES), 1)
    sign = jnp.where(lane < HEAD_DIM, -1.0, 1.0).astype(F32)
    pos_a = jnp.where((lane % AXIAL_DIM) < AXIAL_DIM // 2, t // GRID_W, t % GRID_W)
    ang_a = pos_a.astype(F32) * jnp.tile(inv_freq(AXIAL_DIM), LANES // (AXIAL_DIM // 2))[None, :]
    ang_d = t.astype(F32) * jnp.tile(inv_freq(HEAD_DIM), LANES // (HEAD_DIM // 2))[None, :]
    return jnp.cos(ang_a), jnp.sin(ang_a) * sign, jnp.cos(ang_d), jnp.sin(ang_d) * sign


def _trunk(x, params, proj_tm, mlp_tm, gqa_rows, diff_rows):
    batch, seq, _ = x.shape
    tokens = batch * seq
    (w_in, w_out, attn_norm, qn, kn, lq1, lk1, lq2, lk2, sub_norm, ffn_norm, w_gate_up, w_down,
     final_norm) = params
    depth = w_in.shape[0]
    tables = _rope_tables(seq)
    x2d = x.reshape(tokens, D_MODEL)
    for l in range(depth):
        lambda_init = 0.8 - 0.6 * math.exp(-0.3 * l)
        qg, kg, vg, qd, kd, vd = _proj_call(x2d, seq, l, attn_norm, w_in, qn, kn, *tables, proj_tm)
        shape3 = lambda a: a.reshape(batch, seq, a.shape[-1])
        og = _gqa_call(shape3(qg), shape3(kg), shape3(vg), gqa_rows)
        od = _diff_call(shape3(qd), shape3(kd), shape3(vd), l, lq1, lk1, lq2, lk2, sub_norm, lambda_init,
                        diff_rows)
        x2d = _mlp_call(x2d, og.reshape(tokens, GQA_Q), od.reshape(tokens, DIFF_V), l, w_out, ffn_norm,
                        w_gate_up, w_down, final_norm, l == depth - 1, mlp_tm)
    return x2d.reshape(batch, seq, D_MODEL)


TILES = dict(proj_tm=1024, mlp_tm=512, gqa_rows=128, diff_rows=256)


def _prepare_params(w_in, w_out, attn_norm, gqa_q_norm, gqa_k_norm, diff_lambda_q1, diff_lambda_k1,
                    diff_lambda_q2, diff_lambda_k2, diff_sub_norm, ffn_norm, w_gate_up, w_down, final_norm):
    depth = w_in.shape[0]
    row = lambda a: a.reshape(depth, 1, a.shape[-1])
    tile2 = lambda a: _permute_gqa_columns(jnp.tile(a, (1, LANES // HEAD_DIM))).reshape(depth, 1, LANES)
    return (
        _permute_projection_columns(w_in).astype(BF16), w_out.astype(BF16), row(attn_norm),
        tile2(gqa_q_norm), tile2(gqa_k_norm),
        row(diff_lambda_q1), row(diff_lambda_k1), row(diff_lambda_q2), row(diff_lambda_k2),
        row(diff_sub_norm), row(ffn_norm), w_gate_up.astype(BF16), w_down.astype(BF16),
        final_norm.reshape(1, D_MODEL),
    )


def kernel(x_prompt, x_sample, w_in, w_out, attn_norm, gqa_q_norm, gqa_k_norm, diff_lambda_q1, diff_lambda_k1, diff_lambda_q2, diff_lambda_k2, diff_sub_norm, ffn_norm, w_gate_up, w_down, final_norm):
    params = _prepare_params(w_in, w_out, attn_norm, gqa_q_norm, gqa_k_norm, diff_lambda_q1, diff_lambda_k1,
                             diff_lambda_q2, diff_lambda_k2, diff_sub_norm, ffn_norm, w_gate_up, w_down,
                             final_norm)
    y_prompt = _trunk(x_prompt, params, **TILES)
    y_sample = _trunk(x_sample, params, **TILES)
    return (y_prompt, y_sample)
```

```python
import functools
import math

import jax
import jax.numpy as jnp
from jax import lax
from jax.experimental import pallas as pl
from jax.experimental.pallas import tpu as pltpu

D_MODEL = 1024
GRID_W = 64
HEAD_DIM = 64
AXIAL_DIM = HEAD_DIM // 2
GQA_HEADS = 8
GQA_KV_HEADS = 2
DIFF_HEADS = 4
DIFF_V_DIM = 2 * HEAD_DIM
GQA_Q = GQA_HEADS * HEAD_DIM
GQA_KV = GQA_KV_HEADS * HEAD_DIM
DIFF_QK = DIFF_HEADS * 2 * HEAD_DIM
DIFF_V = DIFF_HEADS * DIFF_V_DIM
IN_WIDTH = GQA_Q + 2 * GQA_KV + 2 * DIFF_QK + DIFF_V
D_FF = 2816
ROPE_THETA = 10000.0
NORM_EPS = 1e-6
DIFF_NORM_EPS = 1e-5

LANES = 128
PROJ_ROW_CHAINS = 2
VMEM_LIMIT_BYTES = 56 * 1024 * 1024
Q_PRESCALE = (HEAD_DIM ** -0.5) * math.log2(math.e)

OFF_QG = 0
OFF_KG = GQA_Q
OFF_VG = OFF_KG + GQA_KV
OFF_QD = OFF_VG + GQA_KV
OFF_KD = OFF_QD + DIFF_QK
OFF_VD = OFF_KD + DIFF_QK

BF16 = jnp.bfloat16
F32 = jnp.float32


def _dot(a, b):
    return jnp.dot(a, b, preferred_element_type=F32)


def _dot_nt(a, b):
    return lax.dot_general(a, b, (((1,), (1,)), ((), ())), preferred_element_type=F32)


def _rms(x, gain, eps):
    ms = jnp.mean(x * x, axis=-1, keepdims=True)
    return x * lax.rsqrt(ms + eps) * gain


def _proj_body(x_ref, an_ref, w_ref, qn_ref, kn_ref, ca_ref, sa_ref, cd_ref, sd_ref,
               qg_ref, kg_ref, vg_ref, qd_ref, kd_ref, vd_ref):
    tm = x_ref.shape[0] // PROJ_ROW_CHAINS
    lane = lax.broadcasted_iota(jnp.int32, (tm, LANES), 1)
    low_half = lane < HEAD_DIM
    first_slot = (lane % HEAD_DIM) < (HEAD_DIM // 2)
    r = (lax.broadcasted_iota(jnp.int32, (2 * LANES, LANES), 0) % HEAD_DIM) // (HEAD_DIM // 2)
    c = (lax.broadcasted_iota(jnp.int32, (2 * LANES, LANES), 1) % HEAD_DIM) // (HEAD_DIM // 2)
    blk = (r == c).astype(BF16)

    def head_norm(xc, gain):
        y = xc * xc
        hi = y.astype(BF16)
        lo = (y - hi.astype(F32)).astype(BF16)
        ss = _dot(jnp.concatenate([hi, lo], axis=1), blk)
        return xc * lax.rsqrt(ss * (1.0 / HEAD_DIM) + NORM_EPS) * gain

    def rope(xc, cos, sin_signed):
        return xc * cos + pltpu.roll(xc, HEAD_DIM, 1) * sin_signed

    zero = jnp.zeros((tm, LANES), F32)
    one = jnp.ones((tm, LANES), F32)

    def _proj_rows(rows):
        h = _rms(x_ref[rows, :], an_ref[...], NORM_EPS).astype(BF16)
        proj = _dot(h, w_ref[...])
        ca, sa, cd, sd = ca_ref[rows, :], sa_ref[rows, :], cd_ref[rows, :], sd_ref[rows, :]

        def project(offset, width):
            return proj[:, offset:offset + width]

        qg = project(OFF_QG, GQA_Q)
        for j in range(GQA_Q // LANES):
            xc = rope(head_norm(qg[:, j * LANES:(j + 1) * LANES], qn_ref[...]), ca, sa) * Q_PRESCALE
            qg_ref[rows, j * LANES:(j + 1) * LANES] = xc.astype(BF16)

        kv = project(OFF_KG, 2 * GQA_KV)
        kc = rope(head_norm(kv[:, :LANES], kn_ref[...]), ca, sa)
        to_first = pltpu.roll(kc, LANES - HEAD_DIM // 2, 1)
        to_second = pltpu.roll(kc, HEAD_DIM // 2, 1)
        kg_ref[rows, 0 * LANES:1 * LANES] = jnp.where(first_slot, kc, zero).astype(BF16)
        kg_ref[rows, 1 * LANES:2 * LANES] = jnp.where(first_slot, zero, to_second).astype(BF16)
        kg_ref[rows, 2 * LANES:3 * LANES] = jnp.where(first_slot, to_first, zero).astype(BF16)
        kg_ref[rows, 3 * LANES:4 * LANES] = jnp.where(first_slot, zero, kc).astype(BF16)

        vc = kv[:, LANES:]
        vg_ref[rows, 0 * LANES:1 * LANES] = jnp.where(low_half, vc, one).astype(BF16)
        vg_ref[rows, 1 * LANES:2 * LANES] = jnp.where(low_half, pltpu.roll(vc, HEAD_DIM, 1), one).astype(BF16)

        qd = project(OFF_QD, DIFF_QK)
        for j in range(DIFF_HEADS):
            xc = rope(qd[:, j * LANES:(j + 1) * LANES], cd, sd) * Q_PRESCALE
            qd_ref[rows, j * LANES:(j + 1) * LANES] = xc.astype(BF16)
        kd = project(OFF_KD, DIFF_QK)
        for j in range(DIFF_HEADS):
            kc = rope(kd[:, j * LANES:(j + 1) * LANES], cd, sd)
            kd_ref[rows, (2 * j) * LANES:(2 * j + 1) * LANES] = jnp.where(first_slot, kc, zero).astype(BF16)
            kd_ref[rows, (2 * j + 1) * LANES:(2 * j + 2) * LANES] = jnp.where(first_slot, zero, kc).astype(BF16)
        vd = project(OFF_VD, DIFF_V)
        for j in range(DIFF_HEADS):
            vd_ref[rows, (2 * j) * LANES:(2 * j + 1) * LANES] = vd[:, j * LANES:(j + 1) * LANES].astype(BF16)
            vd_ref[rows, (2 * j + 1) * LANES:(2 * j + 2) * LANES] = one.astype(BF16)

    for chain in range(PROJ_ROW_CHAINS):
        _proj_rows(pl.ds(chain * tm, tm))


def _proj_call(x2d, seq, layer, attn_norm, w_in, qn, kn, ca, sa, cd, sd, tm):
    tokens = x2d.shape[0]
    n_pos = seq // tm
    tok = lambda i: (i, 0)
    pos = lambda i: (i % n_pos, 0)
    resident = functools.partial(pl.BlockSpec, index_map=lambda i: (layer, 0, 0), pipeline_mode=pl.Buffered(1))
    widths = (GQA_Q, 4 * LANES, 2 * LANES, DIFF_QK, 2 * DIFF_QK, 2 * DIFF_V)
    return pl.pallas_call(
        _proj_body,
        grid=(tokens // tm,),
        in_specs=[
            pl.BlockSpec((tm, D_MODEL), tok),
            resident((None, 1, D_MODEL)),
            resident((None, D_MODEL, IN_WIDTH)),
            resident((None, 1, LANES)),
            resident((None, 1, LANES)),
            pl.BlockSpec((tm, LANES), pos),
            pl.BlockSpec((tm, LANES), pos),
            pl.BlockSpec((tm, LANES), pos),
            pl.BlockSpec((tm, LANES), pos),
        ],
        out_specs=[pl.BlockSpec((tm, w), tok) for w in widths],
        out_shape=[jax.ShapeDtypeStruct((tokens, w), BF16) for w in widths],
        compiler_params=pltpu.CompilerParams(
            dimension_semantics=("parallel",), vmem_limit_bytes=VMEM_LIMIT_BYTES),
        name="proj",
    )(x2d, attn_norm, w_in, qn, kn, ca, sa, cd, sd)


def _softmax_pv(s, v):
    m = jnp.max(s, axis=-1, keepdims=True)
    p = jnp.exp2(s - m).astype(BF16)
    return _dot(p, v)


UNROLLED_SCORES = 8 * 1024 * 1024


def _pipelined_rows(n_blocks, block_scores, scores, consume):
    per_trip = max(2, UNROLLED_SCORES // block_scores // 2 * 2)
    assert per_trip % 2 == 0 and n_blocks >= 2
    scores(0, 0)
    looped = ((n_blocks - 1) // per_trip) * per_trip

    def trip(t, carry):
        for u in range(per_trip):
            i = t * per_trip + u
            scores(i + 1, (u + 1) % 2)
            consume(i, u % 2)
        return carry

    lax.fori_loop(0, looped // per_trip, trip, 0)
    for i in range(looped, n_blocks):
        if i + 1 < n_blocks:
            scores(i + 1, (i + 1) % 2)
        consume(i, i % 2)


def _gqa_body(rows, q_ref, k_ref, v_ref, o_ref, s_scr0, s_scr1):
    s_scr = (s_scr0, s_scr1)
    seq = q_ref.shape[0]
    low_half = lax.broadcasted_iota(jnp.int32, (rows, LANES), 1) < HEAD_DIM

    def scores(i, slot):
        q = q_ref[pl.ds(pl.multiple_of(i * rows, rows), rows), :]
        q2 = jnp.concatenate([q[:, :LANES], q[:, LANES:]], axis=0)
        s_scr[slot][0] = _dot_nt(q2, k_ref[:, :LANES])
        s_scr[slot][1] = _dot_nt(q2, k_ref[:, LANES:])

    def consume(i, slot):
        v = v_ref[...]
        r_lo = _softmax_pv(s_scr[slot][0], v)
        r_hi = _softmax_pv(s_scr[slot][1], v)
        cols = []
        for col in range(2):
            lo = r_lo[col * rows:(col + 1) * rows]
            hi = r_hi[col * rows:(col + 1) * rows]
            lo_sw = pltpu.roll(lo, HEAD_DIM, 1)
            hi_sw = pltpu.roll(hi, HEAD_DIM, 1)
            cols.append(jnp.where(low_half, lo / lo_sw, hi_sw / hi))
        out = jnp.concatenate(cols, axis=1)
        o_ref[pl.ds(pl.multiple_of(i * rows, rows), rows), :] = out.astype(o_ref.dtype)

    _pipelined_rows(seq // rows, (GQA_HEADS // GQA_KV_HEADS) * rows * seq, scores, consume)


def _gqa_call(qg, kg, vg, rows):
    batch, seq, _ = qg.shape
    return pl.pallas_call(
        functools.partial(_gqa_body, rows),
        grid=(batch, GQA_KV_HEADS),
        in_specs=[
            pl.BlockSpec((None, seq, 2 * LANES), lambda b, g: (b, 0, g)),
            pl.BlockSpec((None, seq, 2 * LANES), lambda b, g: (b, 0, g)),
            pl.BlockSpec((None, seq, LANES), lambda b, g: (b, 0, g)),
        ],
        out_specs=pl.BlockSpec((None, seq, 2 * LANES), lambda b, g: (b, 0, g)),
        out_shape=jax.ShapeDtypeStruct((batch, seq, GQA_Q), BF16),
        scratch_shapes=[pltpu.VMEM((2, 2 * rows, seq), F32)] * 2,
        compiler_params=pltpu.CompilerParams(
            dimension_semantics=("parallel", "parallel"),
            vmem_limit_bytes=VMEM_LIMIT_BYTES),
        name="gqa_attn",
    )(qg, kg, vg)


def _diff_body(lambda_init, rows, q_ref, k_ref, v_ref, lq1_ref, lk1_ref, lq2_ref, lk2_ref, sn_ref, o_ref,
               s_scr0, s_scr1):
    s_scr = (s_scr0, s_scr1)
    seq = q_ref.shape[0]
    lam = (jnp.exp(jnp.sum(lq1_ref[...] * lk1_ref[...], axis=-1, keepdims=True))
           - jnp.exp(jnp.sum(lq2_ref[...] * lk2_ref[...], axis=-1, keepdims=True)) + lambda_init)

    def scores(i, slot):
        q = q_ref[pl.ds(pl.multiple_of(i * rows, rows), rows), :]
        s_scr[slot][0] = _dot_nt(q, k_ref[:, :LANES])
        s_scr[slot][1] = _dot_nt(q, k_ref[:, LANES:])

    def consume(i, slot):
        v = v_ref[...]
        r1 = _softmax_pv(s_scr[slot][0], v)
        r2 = _softmax_pv(s_scr[slot][1], v)
        o = r1[:, :LANES] / r1[:, LANES:] - lam * (r2[:, :LANES] / r2[:, LANES:])
        o = _rms(o, sn_ref[...], DIFF_NORM_EPS) * (1.0 - lambda_init)
        o_ref[pl.ds(pl.multiple_of(i * rows, rows), rows), :] = o.astype(o_ref.dtype)

    _pipelined_rows(seq // rows, 2 * rows * seq, scores, consume)


def _diff_call(qd, kd, vd, layer, lq1, lk1, lq2, lk2, sub_norm, lambda_init, rows):
    batch, seq, _ = qd.shape
    const = lambda b, h: (layer, 0, 0)
    return pl.pallas_call(
        functools.partial(_diff_body, lambda_init, rows),
        grid=(batch, DIFF_HEADS),
        in_specs=[
            pl.BlockSpec((None, seq, LANES), lambda b, h: (b, 0, h)),
            pl.BlockSpec((None, seq, 2 * LANES), lambda b, h: (b, 0, h)),
            pl.BlockSpec((None, seq, 2 * LANES), lambda b, h: (b, 0, h)),
            pl.BlockSpec((None, 1, HEAD_DIM), const),
            pl.BlockSpec((None, 1, HEAD_DIM), const),
            pl.BlockSpec((None, 1, HEAD_DIM), const),
            pl.BlockSpec((None, 1, HEAD_DIM), const),
            pl.BlockSpec((None, 1, DIFF_V_DIM), const),
        ],
        out_specs=pl.BlockSpec((None, seq, LANES), lambda b, h: (b, 0, h)),
        out_shape=jax.ShapeDtypeStruct((batch, seq, DIFF_V), BF16),
        scratch_shapes=[pltpu.VMEM((2, rows, seq), F32)] * 2,
        compiler_params=pltpu.CompilerParams(
            dimension_semantics=("parallel", "parallel"),
            vmem_limit_bytes=VMEM_LIMIT_BYTES),
        name="diff_attn",
    )(qd, kd, vd, lq1, lk1, lq2, lk2, sub_norm)


MXU_TILE = 256
FF_SPLITS = (0, 6 * MXU_TILE, D_FF)


def _mlp_body(apply_final_norm, x_ref, og_ref, od_ref, wo_ref, fn_ref, wgu_ref, wd_ref, final_ref, y_ref):
    x = x_ref[...] + (_dot(og_ref[...], wo_ref[:GQA_Q, :]) + _dot(od_ref[...], wo_ref[GQA_Q:, :]))
    h = _rms(x, fn_ref[...], NORM_EPS).astype(BF16)
    ffn = None
    for c0, c1 in zip(FF_SPLITS[:-1], FF_SPLITS[1:]):
        gate = _dot(h, wgu_ref[:, c0:c1])
        up = _dot(h, wgu_ref[:, D_FF + c0:D_FF + c1])
        act = (gate * jax.nn.sigmoid(gate) * up).astype(BF16)
        down = _dot(act, wd_ref[c0:c1, :])
        ffn = down if ffn is None else ffn + down
    x = x + ffn
    if apply_final_norm:
        x = _rms(x, final_ref[...], NORM_EPS)
    y_ref[...] = x


def _mlp_call(x2d, og, od, layer, w_out, ffn_norm, w_gate_up, w_down, final_norm, apply_final_norm, tm):
    tokens = x2d.shape[0]
    tok = lambda i: (i, 0)
    resident = functools.partial(pl.BlockSpec, index_map=lambda i: (layer, 0, 0), pipeline_mode=pl.Buffered(1))
    return pl.pallas_call(
        functools.partial(_mlp_body, apply_final_norm),
        grid=(tokens // tm,),
        in_specs=[
            pl.BlockSpec((tm, D_MODEL), tok),
            pl.BlockSpec((tm, GQA_Q), tok),
            pl.BlockSpec((tm, DIFF_V), tok),
            resident((None, GQA_Q + DIFF_V, D_MODEL)),
            resident((None, 1, D_MODEL)),
            resident((None, D_MODEL, 2 * D_FF)),
            resident((None, D_FF, D_MODEL)),
            pl.BlockSpec((1, D_MODEL), lambda i: (0, 0), pipeline_mode=pl.Buffered(1)),
        ],
        out_specs=pl.BlockSpec((tm, D_MODEL), tok),
        out_shape=jax.ShapeDtypeStruct((tokens, D_MODEL), F32),
        compiler_params=pltpu.CompilerParams(
            dimension_semantics=("parallel",), vmem_limit_bytes=VMEM_LIMIT_BYTES),
        name="mlp",
    )(x2d, og, od, w_out, ffn_norm, w_gate_up, w_down, final_norm)


def _permute_gqa_columns(a):
    lead = a.shape[:-1]
    a = a.reshape(lead + (a.shape[-1] // LANES, 2, 2, 2, AXIAL_DIM // 2))
    n = len(lead)
    a = a.transpose(tuple(range(n)) + (n, n + 3, n + 1, n + 2, n + 4))
    return a.reshape(lead + (-1,))


def _permute_diff_columns(a):
    lead = a.shape[:-1]
    a = a.reshape(lead + (a.shape[-1] // LANES, 2, 2, HEAD_DIM // 2))
    n = len(lead)
    a = a.transpose(tuple(range(n)) + (n, n + 2, n + 1, n + 3))
    return a.reshape(lead + (-1,))


def _permute_projection_columns(w_in):
    return jnp.concatenate([
        _permute_gqa_columns(w_in[..., OFF_QG:OFF_VG]), w_in[..., OFF_VG:OFF_QD],
        _permute_diff_columns(w_in[..., OFF_QD:OFF_VD]), w_in[..., OFF_VD:]], axis=-1)


def _rope_tables(seq):
    def inv_freq(dim):
        return ROPE_THETA ** (-jnp.arange(0, dim, 2, dtype=F32) / dim)

    t = lax.broadcasted_iota(jnp.int32, (seq, LANES), 0)
    lane = lax.broadcasted_iota(jnp.int32, (seq, LANES), 1)
    sign = jnp.where(lane < HEAD_DIM, -1.0, 1.0).astype(F32)
    pos_a = jnp.where((lane % AXIAL_DIM) < AXIAL_DIM // 2, t // GRID_W, t % GRID_W)
    ang_a = pos_a.astype(F32) * jnp.tile(inv_freq(AXIAL_DIM), LANES // (AXIAL_DIM // 2))[None, :]
    ang_d = t.astype(F32) * jnp.tile(inv_freq(HEAD_DIM), LANES // (HEAD_DIM // 2))[None, :]
    return jnp.cos(ang_a), jnp.sin(ang_a) * sign, jnp.cos(ang_d), jnp.sin(ang_d) * sign


def _trunk(x, params, proj_tm, mlp_tm, gqa_rows, diff_rows):
    batch, seq, _ = x.shape
    tokens = batch * seq
    (w_in, w_out, attn_norm, qn, kn, lq1, lk1, lq2, lk2, sub_norm, ffn_norm, w_gate_up, w_down,
     final_norm) = params
    depth = w_in.shape[0]
    tables = _rope_tables(seq)
    x2d = x.reshape(tokens, D_MODEL)
    for l in range(depth):
        lambda_init = 0.8 - 0.6 * math.exp(-0.3 * l)
        qg, kg, vg, qd, kd, vd = _proj_call(x2d, seq, l, attn_norm, w_in, qn, kn, *tables, proj_tm)
        shape3 = lambda a: a.reshape(batch, seq, a.shape[-1])
        og = _gqa_call(shape3(qg), shape3(kg), shape3(vg), gqa_rows)
        od = _diff_call(shape3(qd), shape3(kd), shape3(vd), l, lq1, lk1, lq2, lk2, sub_norm, lambda_init,
                        diff_rows)
        x2d = _mlp_call(x2d, og.reshape(tokens, GQA_Q), od.reshape(tokens, DIFF_V), l, w_out, ffn_norm,
                        w_gate_up, w_down, final_norm, l == depth - 1, mlp_tm)
    return x2d.reshape(batch, seq, D_MODEL)


TILES = dict(proj_tm=1024, mlp_tm=512, gqa_rows=128, diff_rows=256)


def _prepare_params(w_in, w_out, attn_norm, gqa_q_norm, gqa_k_norm, diff_lambda_q1, diff_lambda_k1,
                    diff_lambda_q2, diff_lambda_k2, diff_sub_norm, ffn_norm, w_gate_up, w_down, final_norm):
    depth = w_in.shape[0]
    row = lambda a: a.reshape(depth, 1, a.shape[-1])
    tile2 = lambda a: _permute_gqa_columns(jnp.tile(a, (1, LANES // HEAD_DIM))).reshape(depth, 1, LANES)
    return (
        _permute_projection_columns(w_in).astype(BF16), w_out.astype(BF16), row(attn_norm),
        tile2(gqa_q_norm), tile2(gqa_k_norm),
        row(diff_lambda_q1), row(diff_lambda_k1), row(diff_lambda_q2), row(diff_lambda_k2),
        row(diff_sub_norm), row(ffn_norm), w_gate_up.astype(BF16), w_down.astype(BF16),
        final_norm.reshape(1, D_MODEL),
    )


def kernel(x_prompt, x_sample, w_in, w_out, attn_norm, gqa_q_norm, gqa_k_norm, diff_lambda_q1, diff_lambda_k1, diff_lambda_q2, diff_lambda_k2, diff_sub_norm, ffn_norm, w_gate_up, w_down, final_norm):
    params = _prepare_params(w_in, w_out, attn_norm, gqa_q_norm, gqa_k_norm, diff_lambda_q1, diff_lambda_k1,
                             diff_lambda_q2, diff_lambda_k2, diff_sub_norm, ffn_norm, w_gate_up, w_down,
                             final_norm)
    y_prompt = _trunk(x_prompt, params, **TILES)
    y_sample = _trunk(x_sample, params, **TILES)
    return (y_prompt, y_sample)
```

```python
import functools
import math

import jax
import jax.numpy as jnp
from jax import lax
from jax.experimental import pallas as pl
from jax.experimental.pallas import tpu as pltpu

D_MODEL = 1024
GRID_W = 64
HEAD_DIM = 64
AXIAL_DIM = HEAD_DIM // 2
GQA_HEADS = 8
GQA_KV_HEADS = 2
DIFF_HEADS = 4
DIFF_V_DIM = 2 * HEAD_DIM
GQA_Q = GQA_HEADS * HEAD_DIM
GQA_KV = GQA_KV_HEADS * HEAD_DIM
DIFF_QK = DIFF_HEADS * 2 * HEAD_DIM
DIFF_V = DIFF_HEADS * DIFF_V_DIM
IN_WIDTH = GQA_Q + 2 * GQA_KV + 2 * DIFF_QK + DIFF_V
D_FF = 2816
ROPE_THETA = 10000.0
NORM_EPS = 1e-6
DIFF_NORM_EPS = 1e-5

LANES = 128
PROJ_ROW_CHAINS = 2
VMEM_LIMIT_BYTES = 56 * 1024 * 1024
Q_PRESCALE = (HEAD_DIM ** -0.5) * math.log2(math.e)

OFF_QG = 0
OFF_KG = GQA_Q
OFF_VG = OFF_KG + GQA_KV
OFF_QD = OFF_VG + GQA_KV
OFF_KD = OFF_QD + DIFF_QK
OFF_VD = OFF_KD + DIFF_QK

BF16 = jnp.bfloat16
F32 = jnp.float32


def _dot(a, b):
    return jnp.dot(a, b, preferred_element_type=F32)


def _dot_nt(a, b):
    return lax.dot_general(a, b, (((1,), (1,)), ((), ())), preferred_element_type=F32)


def _rms(x, gain, eps):
    ms = jnp.mean(x * x, axis=-1, keepdims=True)
    return x * lax.rsqrt(ms + eps) * gain


def _proj_body(x_ref, an_ref, w_ref, qn_ref, kn_ref, ca_ref, sa_ref, cd_ref, sd_ref,
               qg_ref, kg_ref, vg_ref, qd_ref, kd_ref, vd_ref):
    tm = x_ref.shape[0] // PROJ_ROW_CHAINS
    lane = lax.broadcasted_iota(jnp.int32, (tm, LANES), 1)
    low_half = lane < HEAD_DIM
    first_slot = (lane % HEAD_DIM) < (HEAD_DIM // 2)
    r = (lax.broadcasted_iota(jnp.int32, (2 * LANES, LANES), 0) % HEAD_DIM) // (HEAD_DIM // 2)
    c = (lax.broadcasted_iota(jnp.int32, (2 * LANES, LANES), 1) % HEAD_DIM) // (HEAD_DIM // 2)
    blk = (r == c).astype(BF16)

    def head_norm(xc, gain):
        y = xc * xc
        hi = y.astype(BF16)
        lo = (y - hi.astype(F32)).astype(BF16)
        ss = _dot(jnp.concatenate([hi, lo], axis=1), blk)
        return xc * lax.rsqrt(ss * (1.0 / HEAD_DIM) + NORM_EPS) * gain

    def rope(xc, cos, sin_signed):
        return xc * cos + pltpu.roll(xc, HEAD_DIM, 1) * sin_signed

    zero = jnp.zeros((tm, LANES), F32)
    one = jnp.ones((tm, LANES), F32)

    def _proj_rows(rows):
        h = _rms(x_ref[rows, :], an_ref[...], NORM_EPS).astype(BF16)
        proj = _dot(h, w_ref[...])
        ca, sa, cd, sd = ca_ref[rows, :], sa_ref[rows, :], cd_ref[rows, :], sd_ref[rows, :]

        def project(offset, width):
            return proj[:, offset:offset + width]

        qg = project(OFF_QG, GQA_Q)
        for j in range(GQA_Q // LANES):
            xc = rope(head_norm(qg[:, j * LANES:(j + 1) * LANES], qn_ref[...]), ca, sa) * Q_PRESCALE
            qg_ref[rows, j * LANES:(j + 1) * LANES] = xc.astype(BF16)

        kv = project(OFF_KG, 2 * GQA_KV)
        kc = rope(head_norm(kv[:, :LANES], kn_ref[...]), ca, sa)
        to_first = pltpu.roll(kc, LANES - HEAD_DIM // 2, 1)
        to_second = pltpu.roll(kc, HEAD_DIM // 2, 1)
        kg_ref[rows, 0 * LANES:1 * LANES] = jnp.where(first_slot, kc, zero).astype(BF16)
        kg_ref[rows, 1 * LANES:2 * LANES] = jnp.where(first_slot, zero, to_second).astype(BF16)
        kg_ref[rows, 2 * LANES:3 * LANES] = jnp.where(first_slot, to_first, zero).astype(BF16)
        kg_ref[rows, 3 * LANES:4 * LANES] = jnp.where(first_slot, zero, kc).astype(BF16)

        vc = kv[:, LANES:]
        vg_ref[rows, 0 * LANES:1 * LANES] = jnp.where(low_half, vc, one).astype(BF16)
        vg_ref[rows, 1 * LANES:2 * LANES] = jnp.where(low_half, pltpu.roll(vc, HEAD_DIM, 1), one).astype(BF16)

        qd = project(OFF_QD, DIFF_QK)
        for j in range(DIFF_HEADS):
            xc = rope(qd[:, j * LANES:(j + 1) * LANES], cd, sd) * Q_PRESCALE
            qd_ref[rows, j * LANES:(j + 1) * LANES] = xc.astype(BF16)
        kd = project(OFF_KD, DIFF_QK)
        for j in range(DIFF_HEADS):
            kc = rope(kd[:, j * LANES:(j + 1) * LANES], cd, sd)
            kd_ref[rows, (2 * j) * LANES:(2 * j + 1) * LANES] = jnp.where(first_slot, kc, zero).astype(BF16)
            kd_ref[rows, (2 * j + 1) * LANES:(2 * j + 2) * LANES] = jnp.where(first_slot, zero, kc).astype(BF16)
        vd = project(OFF_VD, DIFF_V)
        for j in range(DIFF_HEADS):
            vd_ref[rows, (2 * j) * LANES:(2 * j + 1) * LANES] = vd[:, j * LANES:(j + 1) * LANES].astype(BF16)
            vd_ref[rows, (2 * j + 1) * LANES:(2 * j + 2) * LANES] = one.astype(BF16)

    for chain in range(PROJ_ROW_CHAINS):
        _proj_rows(pl.ds(chain * tm, tm))


def _proj_call(x2d, seq, layer, attn_norm, w_in, qn, kn, ca, sa, cd, sd, tm):
    tokens = x2d.shape[0]
    n_pos = seq // tm
    tok = lambda i: (i, 0)
    pos = lambda i: (i % n_pos, 0)
    resident = functools.partial(pl.BlockSpec, index_map=lambda i: (layer, 0, 0), pipeline_mode=pl.Buffered(1))
    widths = (GQA_Q, 4 * LANES, 2 * LANES, DIFF_QK, 2 * DIFF_QK, 2 * DIFF_V)
    return pl.pallas_call(
        _proj_body,
        grid=(tokens // tm,),
        in_specs=[
            pl.BlockSpec((tm, D_MODEL), tok),
            resident((None, 1, D_MODEL)),
            resident((None, D_MODEL, IN_WIDTH)),
            resident((None, 1, LANES)),
            resident((None, 1, LANES)),
            pl.BlockSpec((tm, LANES), pos),
            pl.BlockSpec((tm, LANES), pos),
            pl.BlockSpec((tm, LANES), pos),
            pl.BlockSpec((tm, LANES), pos),
        ],
        out_specs=[pl.BlockSpec((tm, w), tok) for w in widths],
        out_shape=[jax.ShapeDtypeStruct((tokens, w), BF16) for w in widths],
        compiler_params=pltpu.CompilerParams(
            dimension_semantics=("parallel",), vmem_limit_bytes=VMEM_LIMIT_BYTES),
        name="proj",
    )(x2d, attn_norm, w_in, qn, kn, ca, sa, cd, sd)


def _softmax_pv(s, v):
    m = jnp.max(s, axis=-1, keepdims=True)
    p = jnp.exp2(s - m).astype(BF16)
    return _dot(p, v)


UNROLLED_SCORES = 8 * 1024 * 1024


def _pipelined_rows(n_blocks, block_scores, scores, consume):
    per_trip = max(2, UNROLLED_SCORES // block_scores // 2 * 2)
    assert per_trip % 2 == 0 and n_blocks >= 2
    scores(0, 0)
    looped = ((n_blocks - 1) // per_trip) * per_trip

    def trip(t, carry):
        for u in range(per_trip):
            i = t * per_trip + u
            scores(i + 1, (u + 1) % 2)
            consume(i, u % 2)
        return carry

    lax.fori_loop(0, looped // per_trip, trip, 0)
    for i in range(looped, n_blocks):
        if i + 1 < n_blocks:
            scores(i + 1, (i + 1) % 2)
        consume(i, i % 2)


def _gqa_body(rows, q_ref, k_ref, v_ref, o_ref, s_scr0, s_scr1, p_scr0, p_scr1):
    s_scr = (s_scr0, s_scr1)
    p_scr = (p_scr0, p_scr1)
    seq = q_ref.shape[0]
    low_half = lax.broadcasted_iota(jnp.int32, (rows, LANES), 1) < HEAD_DIM

    def scores(i, slot):
        q = q_ref[pl.ds(pl.multiple_of(i * rows, rows), rows), :]
        q2 = jnp.concatenate([q[:, :LANES], q[:, LANES:]], axis=0)
        s_scr[slot][0] = _dot_nt(q2, k_ref[:, :LANES])
        s_scr[slot][1] = _dot_nt(q2, k_ref[:, LANES:])

    def consume(i, slot):
        for m in range(2):
            s = s_scr[slot][m]
            p_scr[slot][m * 2 * rows:(m + 1) * 2 * rows, :] = (
                jnp.exp2(s - jnp.max(s, axis=-1, keepdims=True)).astype(BF16))
        r = _dot(p_scr[slot][...], v_ref[...])
        r_lo, r_hi = r[:2 * rows], r[2 * rows:]
        cols = []
        for col in range(2):
            lo = r_lo[col * rows:(col + 1) * rows]
            hi = r_hi[col * rows:(col + 1) * rows]
            lo_sw = pltpu.roll(lo, HEAD_DIM, 1)
            hi_sw = pltpu.roll(hi, HEAD_DIM, 1)
            cols.append(jnp.where(low_half, lo / lo_sw, hi_sw / hi))
        out = jnp.concatenate(cols, axis=1)
        o_ref[pl.ds(pl.multiple_of(i * rows, rows), rows), :] = out.astype(o_ref.dtype)

    _pipelined_rows(seq // rows, (GQA_HEADS // GQA_KV_HEADS) * rows * seq, scores, consume)


def _gqa_call(qg, kg, vg, rows):
    batch, seq, _ = qg.shape
    return pl.pallas_call(
        functools.partial(_gqa_body, rows),
        grid=(batch, GQA_KV_HEADS),
        in_specs=[
            pl.BlockSpec((None, seq, 2 * LANES), lambda b, g: (b, 0, g)),
            pl.BlockSpec((None, seq, 2 * LANES), lambda b, g: (b, 0, g)),
            pl.BlockSpec((None, seq, LANES), lambda b, g: (b, 0, g)),
        ],
        out_specs=pl.BlockSpec((None, seq, 2 * LANES), lambda b, g: (b, 0, g)),
        out_shape=jax.ShapeDtypeStruct((batch, seq, GQA_Q), BF16),
        scratch_shapes=[pltpu.VMEM((2, 2 * rows, seq), F32)] * 2 + [pltpu.VMEM((4 * rows, seq), BF16)] * 2,
        compiler_params=pltpu.CompilerParams(
            dimension_semantics=("parallel", "parallel"),
            vmem_limit_bytes=VMEM_LIMIT_BYTES),
        name="gqa_attn",
    )(qg, kg, vg)


def _diff_body(lambda_init, rows, q_ref, k_ref, v_ref, lq1_ref, lk1_ref, lq2_ref, lk2_ref, sn_ref, o_ref,
               s_scr0, s_scr1, p_scr0, p_scr1):
    s_scr = (s_scr0, s_scr1)
    p_scr = (p_scr0, p_scr1)
    seq = q_ref.shape[0]
    lam = (jnp.exp(jnp.sum(lq1_ref[...] * lk1_ref[...], axis=-1, keepdims=True))
           - jnp.exp(jnp.sum(lq2_ref[...] * lk2_ref[...], axis=-1, keepdims=True)) + lambda_init)

    def scores(i, slot):
        q = q_ref[pl.ds(pl.multiple_of(i * rows, rows), rows), :]
        s_scr[slot][0] = _dot_nt(q, k_ref[:, :LANES])
        s_scr[slot][1] = _dot_nt(q, k_ref[:, LANES:])

    def consume(i, slot):
        for m in range(2):
            s = s_scr[slot][m]
            p_scr[slot][m * rows:(m + 1) * rows, :] = jnp.exp2(s - jnp.max(s, axis=-1, keepdims=True)).astype(BF16)
        r = _dot(p_scr[slot][...], v_ref[...])
        r1, r2 = r[:rows], r[rows:]
        o = r1[:, :LANES] / r1[:, LANES:] - lam * (r2[:, :LANES] / r2[:, LANES:])
        o = _rms(o, sn_ref[...], DIFF_NORM_EPS) * (1.0 - lambda_init)
        o_ref[pl.ds(pl.multiple_of(i * rows, rows), rows), :] = o.astype(o_ref.dtype)

    _pipelined_rows(seq // rows, 2 * rows * seq, scores, consume)


def _diff_call(qd, kd, vd, layer, lq1, lk1, lq2, lk2, sub_norm, lambda_init, rows):
    batch, seq, _ = qd.shape
    const = lambda b, h: (layer, 0, 0)
    return pl.pallas_call(
        functools.partial(_diff_body, lambda_init, rows),
        grid=(batch, DIFF_HEADS),
        in_specs=[
            pl.BlockSpec((None, seq, LANES), lambda b, h: (b, 0, h)),
            pl.BlockSpec((None, seq, 2 * LANES), lambda b, h: (b, 0, h)),
            pl.BlockSpec((None, seq, 2 * LANES), lambda b, h: (b, 0, h)),
            pl.BlockSpec((None, 1, HEAD_DIM), const),
            pl.BlockSpec((None, 1, HEAD_DIM), const),
            pl.BlockSpec((None, 1, HEAD_DIM), const),
            pl.BlockSpec((None, 1, HEAD_DIM), const),
            pl.BlockSpec((None, 1, DIFF_V_DIM), const),
        ],
        out_specs=pl.BlockSpec((None, seq, LANES), lambda b, h: (b, 0, h)),
        out_shape=jax.ShapeDtypeStruct((batch, seq, DIFF_V), BF16),
        scratch_shapes=[pltpu.VMEM((2, rows, seq), F32)] * 2 + [pltpu.VMEM((2 * rows, seq), BF16)] * 2,
        compiler_params=pltpu.CompilerParams(
            dimension_semantics=("parallel", "parallel"),
            vmem_limit_bytes=VMEM_LIMIT_BYTES),
        name="diff_attn",
    )(qd, kd, vd, lq1, lk1, lq2, lk2, sub_norm)


MXU_TILE = 256
FF_SPLITS = (0, 6 * MXU_TILE, D_FF)


def _mlp_body(apply_final_norm, x_ref, og_ref, od_ref, wo_ref, fn_ref, wgu_ref, wd_ref, final_ref, y_ref):
    x = x_ref[...] + (_dot(og_ref[...], wo_ref[:GQA_Q, :]) + _dot(od_ref[...], wo_ref[GQA_Q:, :]))
    h = _rms(x, fn_ref[...], NORM_EPS).astype(BF16)
    ffn = None
    for c0, c1 in zip(FF_SPLITS[:-1], FF_SPLITS[1:]):
        gate = _dot(h, wgu_ref[:, c0:c1])
        up = _dot(h, wgu_ref[:, D_FF + c0:D_FF + c1])
        act = (gate * jax.nn.sigmoid(gate) * up).astype(BF16)
        down = _dot(act, wd_ref[c0:c1, :])
        ffn = down if ffn is None else ffn + down
    x = x + ffn
    if apply_final_norm:
        x = _rms(x, final_ref[...], NORM_EPS)
    y_ref[...] = x


def _mlp_call(x2d, og, od, layer, w_out, ffn_norm, w_gate_up, w_down, final_norm, apply_final_norm, tm):
    tokens = x2d.shape[0]
    tok = lambda i: (i, 0)
    resident = functools.partial(pl.BlockSpec, index_map=lambda i: (layer, 0, 0), pipeline_mode=pl.Buffered(1))
    return pl.pallas_call(
        functools.partial(_mlp_body, apply_final_norm),
        grid=(tokens // tm,),
        in_specs=[
            pl.BlockSpec((tm, D_MODEL), tok),
            pl.BlockSpec((tm, GQA_Q), tok),
            pl.BlockSpec((tm, DIFF_V), tok),
            resident((None, GQA_Q + DIFF_V, D_MODEL)),
            resident((None, 1, D_MODEL)),
            resident((None, D_MODEL, 2 * D_FF)),
            resident((None, D_FF, D_MODEL)),
            pl.BlockSpec((1, D_MODEL), lambda i: (0, 0), pipeline_mode=pl.Buffered(1)),
        ],
        out_specs=pl.BlockSpec((tm, D_MODEL), tok),
        out_shape=jax.ShapeDtypeStruct((tokens, D_MODEL), F32),
        compiler_params=pltpu.CompilerParams(
            dimension_semantics=("parallel",), vmem_limit_bytes=VMEM_LIMIT_BYTES),
        name="mlp",
    )(x2d, og, od, w_out, ffn_norm, w_gate_up, w_down, final_norm)


def _permute_gqa_columns(a):
    lead = a.shape[:-1]
    a = a.reshape(lead + (a.shape[-1] // LANES, 2, 2, 2, AXIAL_DIM // 2))
    n = len(lead)
    a = a.transpose(tuple(range(n)) + (n, n + 3, n + 1, n + 2, n + 4))
    return a.reshape(lead + (-1,))


def _permute_diff_columns(a):
    lead = a.shape[:-1]
    a = a.reshape(lead + (a.shape[-1] // LANES, 2, 2, HEAD_DIM // 2))
    n = len(lead)
    a = a.transpose(tuple(range(n)) + (n, n + 2, n + 1, n + 3))
    return a.reshape(lead + (-1,))


def _permute_projection_columns(w_in):
    return jnp.concatenate([
        _permute_gqa_columns(w_in[..., OFF_QG:OFF_VG]), w_in[..., OFF_VG:OFF_QD],
        _permute_diff_columns(w_in[..., OFF_QD:OFF_VD]), w_in[..., OFF_VD:]], axis=-1)


def _rope_tables(seq):
    def inv_freq(dim):
        return ROPE_THETA ** (-jnp.arange(0, dim, 2, dtype=F32) / dim)

    t = lax.broadcasted_iota(jnp.int32, (seq, LANES), 0)
    lane = lax.broadcasted_iota(jnp.int32, (seq, LANES), 1)
    sign = jnp.where(lane < HEAD_DIM, -1.0, 1.0).astype(F32)
    pos_a = jnp.where((lane % AXIAL_DIM) < AXIAL_DIM // 2, t // GRID_W, t % GRID_W)
    ang_a = pos_a.astype(F32) * jnp.tile(inv_freq(AXIAL_DIM), LANES // (AXIAL_DIM // 2))[None, :]
    ang_d = t.astype(F32) * jnp.tile(inv_freq(HEAD_DIM), LANES // (HEAD_DIM // 2))[None, :]
    return jnp.cos(ang_a), jnp.sin(ang_a) * sign, jnp.cos(ang_d), jnp.sin(ang_d) * sign


def _trunk(x, params, proj_tm, mlp_tm, gqa_rows, diff_rows):
    batch, seq, _ = x.shape
    tokens = batch * seq
    (w_in, w_out, attn_norm, qn, kn, lq1, lk1, lq2, lk2, sub_norm, ffn_norm, w_gate_up, w_down,
     final_norm) = params
    depth = w_in.shape[0]
    tables = _rope_tables(seq)
    x2d = x.reshape(tokens, D_MODEL)
    for l in range(depth):
        lambda_init = 0.8 - 0.6 * math.exp(-0.3 * l)
        qg, kg, vg, qd, kd, vd = _proj_call(x2d, seq, l, attn_norm, w_in, qn, kn, *tables, proj_tm)
        shape3 = lambda a: a.reshape(batch, seq, a.shape[-1])
        og = _gqa_call(shape3(qg), shape3(kg), shape3(vg), gqa_rows)
        od = _diff_call(shape3(qd), shape3(kd), shape3(vd), l, lq1, lk1, lq2, lk2, sub_norm, lambda_init,
                        diff_rows)
        x2d = _mlp_call(x2d, og.reshape(tokens, GQA_Q), od.reshape(tokens, DIFF_V), l, w_out, ffn_norm,
                        w_gate_up, w_down, final_norm, l == depth - 1, mlp_tm)
    return x2d.reshape(batch, seq, D_MODEL)


TILES = dict(proj_tm=1024, mlp_tm=512, gqa_rows=128, diff_rows=256)


def _prepare_params(w_in, w_out, attn_norm, gqa_q_norm, gqa_k_norm, diff_lambda_q1, diff_lambda_k1,
                    diff_lambda_q2, diff_lambda_k2, diff_sub_norm, ffn_norm, w_gate_up, w_down, final_norm):
    depth = w_in.shape[0]
    row = lambda a: a.reshape(depth, 1, a.shape[-1])
    tile2 = lambda a: _permute_gqa_columns(jnp.tile(a, (1, LANES // HEAD_DIM))).reshape(depth, 1, LANES)
    return (
        _permute_projection_columns(w_in).astype(BF16), w_out.astype(BF16), row(attn_norm),
        tile2(gqa_q_norm), tile2(gqa_k_norm),
        row(diff_lambda_q1), row(diff_lambda_k1), row(diff_lambda_q2), row(diff_lambda_k2),
        row(diff_sub_norm), row(ffn_norm), w_gate_up.astype(BF16), w_down.astype(BF16),
        final_norm.reshape(1, D_MODEL),
    )


def kernel(x_prompt, x_sample, w_in, w_out, attn_norm, gqa_q_norm, gqa_k_norm, diff_lambda_q1, diff_lambda_k1, diff_lambda_q2, diff_lambda_k2, diff_sub_norm, ffn_norm, w_gate_up, w_down, final_norm):
    params = _prepare_params(w_in, w_out, attn_norm, gqa_q_norm, gqa_k_norm, diff_lambda_q1, diff_lambda_k1,
                             diff_lambda_q2, diff_lambda_k2, diff_sub_norm, ffn_norm, w_gate_up, w_down,
                             final_norm)
    y_prompt = _trunk(x_prompt, params, **TILES)
    y_sample = _trunk(x_sample, params, **TILES)
    return (y_prompt, y_sample)
```

```python
import functools
import math

import jax
import jax.numpy as jnp
from jax import lax
from jax.experimental import pallas as pl
from jax.experimental.pallas import tpu as pltpu

D_MODEL = 1024
GRID_W = 64
HEAD_DIM = 64
AXIAL_DIM = HEAD_DIM // 2
GQA_HEADS = 8
GQA_KV_HEADS = 2
DIFF_HEADS = 4
DIFF_V_DIM = 2 * HEAD_DIM
GQA_Q = GQA_HEADS * HEAD_DIM
GQA_KV = GQA_KV_HEADS * HEAD_DIM
DIFF_QK = DIFF_HEADS * 2 * HEAD_DIM
DIFF_V = DIFF_HEADS * DIFF_V_DIM
IN_WIDTH = GQA_Q + 2 * GQA_KV + 2 * DIFF_QK + DIFF_V
D_FF = 2816
ROPE_THETA = 10000.0
NORM_EPS = 1e-6
DIFF_NORM_EPS = 1e-5

LANES = 128
PROJ_ROW_CHAINS = 2
VMEM_LIMIT_BYTES = 56 * 1024 * 1024
Q_PRESCALE = (HEAD_DIM ** -0.5) * math.log2(math.e)

OFF_QG = 0
OFF_KG = GQA_Q
OFF_VG = OFF_KG + GQA_KV
OFF_QD = OFF_VG + GQA_KV
OFF_KD = OFF_QD + DIFF_QK
OFF_VD = OFF_KD + DIFF_QK

BF16 = jnp.bfloat16
F32 = jnp.float32


def _dot(a, b):
    return jnp.dot(a, b, preferred_element_type=F32)


def _dot_nt(a, b):
    return lax.dot_general(a, b, (((1,), (1,)), ((), ())), preferred_element_type=F32)


def _rms(x, gain, eps):
    ms = jnp.mean(x * x, axis=-1, keepdims=True)
    return x * lax.rsqrt(ms + eps) * gain


def _proj_body(x_ref, an_ref, w_ref, qn_ref, kn_ref, ca_ref, sa_ref, cd_ref, sd_ref,
               qg_ref, kg_ref, vg_ref, qd_ref, kd_ref, vd_ref):
    tm = x_ref.shape[0] // PROJ_ROW_CHAINS
    lane = lax.broadcasted_iota(jnp.int32, (tm, LANES), 1)
    low_half = lane < HEAD_DIM
    first_slot = (lane % HEAD_DIM) < (HEAD_DIM // 2)
    r = (lax.broadcasted_iota(jnp.int32, (2 * LANES, LANES), 0) % HEAD_DIM) // (HEAD_DIM // 2)
    c = (lax.broadcasted_iota(jnp.int32, (2 * LANES, LANES), 1) % HEAD_DIM) // (HEAD_DIM // 2)
    blk = (r == c).astype(BF16)

    def head_norm(xc, gain):
        y = xc * xc
        hi = y.astype(BF16)
        lo = (y - hi.astype(F32)).astype(BF16)
        ss = _dot(jnp.concatenate([hi, lo], axis=1), blk)
        return xc * lax.rsqrt(ss * (1.0 / HEAD_DIM) + NORM_EPS) * gain

    def rope(xc, cos, sin_signed):
        return xc * cos + pltpu.roll(xc, HEAD_DIM, 1) * sin_signed

    zero = jnp.zeros((tm, LANES), F32)
    one = jnp.ones((tm, LANES), F32)

    def _proj_rows(rows):
        h = _rms(x_ref[rows, :], an_ref[...], NORM_EPS).astype(BF16)
        proj = _dot(h, w_ref[...])
        ca, sa, cd, sd = ca_ref[rows, :], sa_ref[rows, :], cd_ref[rows, :], sd_ref[rows, :]

        def project(offset, width):
            return proj[:, offset:offset + width]

        qg = project(OFF_QG, GQA_Q)
        for j in range(GQA_Q // LANES):
            xc = rope(head_norm(qg[:, j * LANES:(j + 1) * LANES], qn_ref[...]), ca, sa) * Q_PRESCALE
            qg_ref[rows, j * LANES:(j + 1) * LANES] = xc.astype(BF16)

        kv = project(OFF_KG, 2 * GQA_KV)
        kc = rope(head_norm(kv[:, :LANES], kn_ref[...]), ca, sa)
        to_first = pltpu.roll(kc, LANES - HEAD_DIM // 2, 1)
        to_second = pltpu.roll(kc, HEAD_DIM // 2, 1)
        kg_ref[rows, 0 * LANES:1 * LANES] = jnp.where(first_slot, kc, zero).astype(BF16)
        kg_ref[rows, 1 * LANES:2 * LANES] = jnp.where(first_slot, zero, to_second).astype(BF16)
        kg_ref[rows, 2 * LANES:3 * LANES] = jnp.where(first_slot, to_first, zero).astype(BF16)
        kg_ref[rows, 3 * LANES:4 * LANES] = jnp.where(first_slot, zero, kc).astype(BF16)

        vc = kv[:, LANES:]
        vg_ref[rows, 0 * LANES:1 * LANES] = jnp.where(low_half, vc, one).astype(BF16)
        vg_ref[rows, 1 * LANES:2 * LANES] = jnp.where(low_half, pltpu.roll(vc, HEAD_DIM, 1), one).astype(BF16)

        qd = project(OFF_QD, DIFF_QK)
        for j in range(DIFF_HEADS):
            xc = rope(qd[:, j * LANES:(j + 1) * LANES], cd, sd) * Q_PRESCALE
            qd_ref[rows, j * LANES:(j + 1) * LANES] = xc.astype(BF16)
        kd = project(OFF_KD, DIFF_QK)
        for j in range(DIFF_HEADS):
            kc = rope(kd[:, j * LANES:(j + 1) * LANES], cd, sd)
            kd_ref[rows, (2 * j) * LANES:(2 * j + 1) * LANES] = jnp.where(first_slot, kc, zero).astype(BF16)
            kd_ref[rows, (2 * j + 1) * LANES:(2 * j + 2) * LANES] = jnp.where(first_slot, zero, kc).astype(BF16)
        vd = project(OFF_VD, DIFF_V)
        for j in range(DIFF_HEADS):
            vd_ref[rows, (2 * j) * LANES:(2 * j + 1) * LANES] = vd[:, j * LANES:(j + 1) * LANES].astype(BF16)
            vd_ref[rows, (2 * j + 1) * LANES:(2 * j + 2) * LANES] = one.astype(BF16)

    for chain in range(PROJ_ROW_CHAINS):
        _proj_rows(pl.ds(chain * tm, tm))


def _proj_call(x2d, seq, layer, attn_norm, w_in, qn, kn, ca, sa, cd, sd, tm):
    tokens = x2d.shape[0]
    n_pos = seq // tm
    tok = lambda i: (i, 0)
    pos = lambda i: (i % n_pos, 0)
    resident = functools.partial(pl.BlockSpec, index_map=lambda i: (layer, 0, 0), pipeline_mode=pl.Buffered(1))
    widths = (GQA_Q, 4 * LANES, 2 * LANES, DIFF_QK, 2 * DIFF_QK, 2 * DIFF_V)
    return pl.pallas_call(
        _proj_body,
        grid=(tokens // tm,),
        in_specs=[
            pl.BlockSpec((tm, D_MODEL), tok),
            resident((None, 1, D_MODEL)),
            resident((None, D_MODEL, IN_WIDTH)),
            resident((None, 1, LANES)),
            resident((None, 1, LANES)),
            pl.BlockSpec((tm, LANES), pos),
            pl.BlockSpec((tm, LANES), pos),
            pl.BlockSpec((tm, LANES), pos),
            pl.BlockSpec((tm, LANES), pos),
        ],
        out_specs=[pl.BlockSpec((tm, w), tok) for w in widths],
        out_shape=[jax.ShapeDtypeStruct((tokens, w), BF16) for w in widths],
        compiler_params=pltpu.CompilerParams(
            dimension_semantics=("parallel",), vmem_limit_bytes=VMEM_LIMIT_BYTES),
        name="proj",
    )(x2d, attn_norm, w_in, qn, kn, ca, sa, cd, sd)


UNROLLED_SCORES = 8 * 1024 * 1024


def _pipelined_rows(n_blocks, block_scores, scores, consume):
    per_trip = max(2, UNROLLED_SCORES // block_scores // 2 * 2)
    assert per_trip % 2 == 0 and n_blocks >= 2
    scores(0, 0)
    looped = ((n_blocks - 1) // per_trip) * per_trip

    def trip(t, carry):
        for u in range(per_trip):
            i = t * per_trip + u
            scores(i + 1, (u + 1) % 2)
            consume(i, u % 2)
        return carry

    lax.fori_loop(0, looped // per_trip, trip, 0)
    for i in range(looped, n_blocks):
        if i + 1 < n_blocks:
            scores(i + 1, (i + 1) % 2)
        consume(i, i % 2)


def _gqa_body(rows, q_ref, k_ref, v_ref, o_ref, s_scr0, s_scr1, p_scr0, p_scr1):
    s_scr = (s_scr0, s_scr1)
    p_scr = (p_scr0, p_scr1)
    seq = q_ref.shape[0]
    low_half = lax.broadcasted_iota(jnp.int32, (rows, LANES), 1) < HEAD_DIM

    def scores(i, slot):
        q = q_ref[pl.ds(pl.multiple_of(i * rows, rows), rows), :]
        q2 = jnp.concatenate([q[:, :LANES], q[:, LANES:]], axis=0)
        s_scr[slot][0] = _dot_nt(q2, k_ref[:, :LANES])
        s_scr[slot][1] = _dot_nt(q2, k_ref[:, LANES:])

    def consume(i, slot):
        for m in range(2):
            s = s_scr[slot][m]
            p_scr[slot][m * 2 * rows:(m + 1) * 2 * rows, :] = (
                jnp.exp2(s - jnp.max(s, axis=-1, keepdims=True)).astype(BF16))
        r = _dot(p_scr[slot][...], v_ref[...])
        r_lo, r_hi = r[:2 * rows], r[2 * rows:]
        cols = []
        for col in range(2):
            lo = r_lo[col * rows:(col + 1) * rows]
            hi = r_hi[col * rows:(col + 1) * rows]
            lo_sw = pltpu.roll(lo, HEAD_DIM, 1)
            hi_sw = pltpu.roll(hi, HEAD_DIM, 1)
            cols.append(jnp.where(low_half, lo / lo_sw, hi_sw / hi))
        out = jnp.concatenate(cols, axis=1)
        o_ref[pl.ds(pl.multiple_of(i * rows, rows), rows), :] = out.astype(o_ref.dtype)

    _pipelined_rows(seq // rows, (GQA_HEADS // GQA_KV_HEADS) * rows * seq, scores, consume)


def _gqa_call(qg, kg, vg, rows):
    batch, seq, _ = qg.shape
    return pl.pallas_call(
        functools.partial(_gqa_body, rows),
        grid=(batch, GQA_KV_HEADS),
        in_specs=[
            pl.BlockSpec((None, seq, 2 * LANES), lambda b, g: (b, 0, g)),
            pl.BlockSpec((None, seq, 2 * LANES), lambda b, g: (b, 0, g)),
            pl.BlockSpec((None, seq, LANES), lambda b, g: (b, 0, g)),
        ],
        out_specs=pl.BlockSpec((None, seq, 2 * LANES), lambda b, g: (b, 0, g)),
        out_shape=jax.ShapeDtypeStruct((batch, seq, GQA_Q), BF16),
        scratch_shapes=[pltpu.VMEM((2, 2 * rows, seq), F32)] * 2 + [pltpu.VMEM((4 * rows, seq), BF16)] * 2,
        compiler_params=pltpu.CompilerParams(
            dimension_semantics=("parallel", "parallel"),
            vmem_limit_bytes=VMEM_LIMIT_BYTES),
        name="gqa_attn",
    )(qg, kg, vg)


def _diff_body(lambda_init, rows, q_ref, k_ref, v_ref, lq1_ref, lk1_ref, lq2_ref, lk2_ref, sn_ref, o_ref,
               s_scr0, s_scr1, p_scr0, p_scr1):
    s_scr = (s_scr0, s_scr1)
    p_scr = (p_scr0, p_scr1)
    seq = q_ref.shape[0]
    lam = (jnp.exp(jnp.sum(lq1_ref[...] * lk1_ref[...], axis=-1, keepdims=True))
           - jnp.exp(jnp.sum(lq2_ref[...] * lk2_ref[...], axis=-1, keepdims=True)) + lambda_init)

    def scores(i, slot):
        q = q_ref[pl.ds(pl.multiple_of(i * rows, rows), rows), :]
        s_scr[slot][0] = _dot_nt(q, k_ref[:, :LANES])
        s_scr[slot][1] = _dot_nt(q, k_ref[:, LANES:])

    def consume(i, slot):
        for m in range(2):
            s = s_scr[slot][m]
            p_scr[slot][m * rows:(m + 1) * rows, :] = jnp.exp2(s - jnp.max(s, axis=-1, keepdims=True)).astype(BF16)
        r = _dot(p_scr[slot][...], v_ref[...])
        r1, r2 = r[:rows], r[rows:]
        o = r1[:, :LANES] / r1[:, LANES:] - lam * (r2[:, :LANES] / r2[:, LANES:])
        o = _rms(o, sn_ref[...], DIFF_NORM_EPS) * (1.0 - lambda_init)
        o_ref[pl.ds(pl.multiple_of(i * rows, rows), rows), :] = o.astype(o_ref.dtype)

    _pipelined_rows(seq // rows, 2 * rows * seq, scores, consume)


def _diff_call(qd, kd, vd, layer, lq1, lk1, lq2, lk2, sub_norm, lambda_init, rows):
    batch, seq, _ = qd.shape
    const = lambda b, h: (layer, 0, 0)
    return pl.pallas_call(
        functools.partial(_diff_body, lambda_init, rows),
        grid=(batch, DIFF_HEADS),
        in_specs=[
            pl.BlockSpec((None, seq, LANES), lambda b, h: (b, 0, h)),
            pl.BlockSpec((None, seq, 2 * LANES), lambda b, h: (b, 0, h)),
            pl.BlockSpec((None, seq, 2 * LANES), lambda b, h: (b, 0, h)),
            pl.BlockSpec((None, 1, HEAD_DIM), const),
            pl.BlockSpec((None, 1, HEAD_DIM), const),
            pl.BlockSpec((None, 1, HEAD_DIM), const),
            pl.BlockSpec((None, 1, HEAD_DIM), const),
            pl.BlockSpec((None, 1, DIFF_V_DIM), const),
        ],
        out_specs=pl.BlockSpec((None, seq, LANES), lambda b, h: (b, 0, h)),
        out_shape=jax.ShapeDtypeStruct((batch, seq, DIFF_V), BF16),
        scratch_shapes=[pltpu.VMEM((2, rows, seq), F32)] * 2 + [pltpu.VMEM((2 * rows, seq), BF16)] * 2,
        compiler_params=pltpu.CompilerParams(
            dimension_semantics=("parallel", "parallel"),
            vmem_limit_bytes=VMEM_LIMIT_BYTES),
        name="diff_attn",
    )(qd, kd, vd, lq1, lk1, lq2, lk2, sub_norm)


MXU_TILE = 256
FF_SPLITS = (0, 6 * MXU_TILE, D_FF)


def _mlp_body(apply_final_norm, x_ref, og_ref, od_ref, wo_ref, fn_ref, wgu_ref, wd_ref, final_ref, y_ref):
    x = x_ref[...] + (_dot(og_ref[...], wo_ref[:GQA_Q, :]) + _dot(od_ref[...], wo_ref[GQA_Q:, :]))
    h = _rms(x, fn_ref[...], NORM_EPS).astype(BF16)
    ffn = None
    for c0, c1 in zip(FF_SPLITS[:-1], FF_SPLITS[1:]):
        gate = _dot(h, wgu_ref[:, c0:c1])
        up = _dot(h, wgu_ref[:, D_FF + c0:D_FF + c1])
        act = (gate * jax.nn.sigmoid(gate) * up).astype(BF16)
        down = _dot(act, wd_ref[c0:c1, :])
        ffn = down if ffn is None else ffn + down
    x = x + ffn
    if apply_final_norm:
        x = _rms(x, final_ref[...], NORM_EPS)
    y_ref[...] = x


def _mlp_call(x2d, og, od, layer, w_out, ffn_norm, w_gate_up, w_down, final_norm, apply_final_norm, tm):
    tokens = x2d.shape[0]
    tok = lambda i: (i, 0)
    resident = functools.partial(pl.BlockSpec, index_map=lambda i: (layer, 0, 0), pipeline_mode=pl.Buffered(1))
    return pl.pallas_call(
        functools.partial(_mlp_body, apply_final_norm),
        grid=(tokens // tm,),
        in_specs=[
            pl.BlockSpec((tm, D_MODEL), tok),
            pl.BlockSpec((tm, GQA_Q), tok),
            pl.BlockSpec((tm, DIFF_V), tok),
            resident((None, GQA_Q + DIFF_V, D_MODEL)),
            resident((None, 1, D_MODEL)),
            resident((None, D_MODEL, 2 * D_FF)),
            resident((None, D_FF, D_MODEL)),
            pl.BlockSpec((1, D_MODEL), lambda i: (0, 0), pipeline_mode=pl.Buffered(1)),
        ],
        out_specs=pl.BlockSpec((tm, D_MODEL), tok),
        out_shape=jax.ShapeDtypeStruct((tokens, D_MODEL), F32),
        compiler_params=pltpu.CompilerParams(
            dimension_semantics=("parallel",), vmem_limit_bytes=VMEM_LIMIT_BYTES),
        name="mlp",
    )(x2d, og, od, w_out, ffn_norm, w_gate_up, w_down, final_norm)


def _permute_gqa_columns(a):
    lead = a.shape[:-1]
    a = a.reshape(lead + (a.shape[-1] // LANES, 2, 2, 2, AXIAL_DIM // 2))
    n = len(lead)
    a = a.transpose(tuple(range(n)) + (n, n + 3, n + 1, n + 2, n + 4))
    return a.reshape(lead + (-1,))


def _permute_diff_columns(a):
    lead = a.shape[:-1]
    a = a.reshape(lead + (a.shape[-1] // LANES, 2, 2, HEAD_DIM // 2))
    n = len(lead)
    a = a.transpose(tuple(range(n)) + (n, n + 2, n + 1, n + 3))
    return a.reshape(lead + (-1,))


def _permute_projection_columns(w_in):
    return jnp.concatenate([
        _permute_gqa_columns(w_in[..., OFF_QG:OFF_VG]), w_in[..., OFF_VG:OFF_QD],
        _permute_diff_columns(w_in[..., OFF_QD:OFF_VD]), w_in[..., OFF_VD:]], axis=-1)


def _rope_tables(seq):
    def inv_freq(dim):
        return ROPE_THETA ** (-jnp.arange(0, dim, 2, dtype=F32) / dim)

    t = lax.broadcasted_iota(jnp.int32, (seq, LANES), 0)
    lane = lax.broadcasted_iota(jnp.int32, (seq, LANES), 1)
    sign = jnp.where(lane < HEAD_DIM, -1.0, 1.0).astype(F32)
    pos_a = jnp.where((lane % AXIAL_DIM) < AXIAL_DIM // 2, t // GRID_W, t % GRID_W)
    ang_a = pos_a.astype(F32) * jnp.tile(inv_freq(AXIAL_DIM), LANES // (AXIAL_DIM // 2))[None, :]
    ang_d = t.astype(F32) * jnp.tile(inv_freq(HEAD_DIM), LANES // (HEAD_DIM // 2))[None, :]
    return jnp.cos(ang_a), jnp.sin(ang_a) * sign, jnp.cos(ang_d), jnp.sin(ang_d) * sign


def _trunk(x, params, proj_tm, mlp_tm, block_scores):
    batch, seq, _ = x.shape
    tokens = batch * seq
    gqa_rows = block_scores // ((GQA_HEADS // GQA_KV_HEADS) * seq)
    diff_rows = block_scores // (2 * seq)
    (w_in, w_out, attn_norm, qn, kn, lq1, lk1, lq2, lk2, sub_norm, ffn_norm, w_gate_up, w_down,
     final_norm) = params
    depth = w_in.shape[0]
    tables = _rope_tables(seq)
    x2d = x.reshape(tokens, D_MODEL)
    for l in range(depth):
        lambda_init = 0.8 - 0.6 * math.exp(-0.3 * l)
        qg, kg, vg, qd, kd, vd = _proj_call(x2d, seq, l, attn_norm, w_in, qn, kn, *tables, proj_tm)
        shape3 = lambda a: a.reshape(batch, seq, a.shape[-1])
        og = _gqa_call(shape3(qg), shape3(kg), shape3(vg), gqa_rows)
        od = _diff_call(shape3(qd), shape3(kd), shape3(vd), l, lq1, lk1, lq2, lk2, sub_norm, lambda_init,
                        diff_rows)
        x2d = _mlp_call(x2d, og.reshape(tokens, GQA_Q), od.reshape(tokens, DIFF_V), l, w_out, ffn_norm,
                        w_gate_up, w_down, final_norm, l == depth - 1, mlp_tm)
    return x2d.reshape(batch, seq, D_MODEL)


TILES = dict(proj_tm=1024, mlp_tm=512, block_scores=2 * 1024 * 1024)


def _prepare_params(w_in, w_out, attn_norm, gqa_q_norm, gqa_k_norm, diff_lambda_q1, diff_lambda_k1,
                    diff_lambda_q2, diff_lambda_k2, diff_sub_norm, ffn_norm, w_gate_up, w_down, final_norm):
    depth = w_in.shape[0]
    row = lambda a: a.reshape(depth, 1, a.shape[-1])
    tile2 = lambda a: _permute_gqa_columns(jnp.tile(a, (1, LANES // HEAD_DIM))).reshape(depth, 1, LANES)
    return (
        _permute_projection_columns(w_in).astype(BF16), w_out.astype(BF16), row(attn_norm),
        tile2(gqa_q_norm), tile2(gqa_k_norm),
        row(diff_lambda_q1), row(diff_lambda_k1), row(diff_lambda_q2), row(diff_lambda_k2),
        row(diff_sub_norm), row(ffn_norm), w_gate_up.astype(BF16), w_down.astype(BF16),
        final_norm.reshape(1, D_MODEL),
    )


def kernel(x_prompt, x_sample, w_in, w_out, attn_norm, gqa_q_norm, gqa_k_norm, diff_lambda_q1, diff_lambda_k1, diff_lambda_q2, diff_lambda_k2, diff_sub_norm, ffn_norm, w_gate_up, w_down, final_norm):
    params = _prepare_params(w_in, w_out, attn_norm, gqa_q_norm, gqa_k_norm, diff_lambda_q1, diff_lambda_k1,
                             diff_lambda_q2, diff_lambda_k2, diff_sub_norm, ffn_norm, w_gate_up, w_down,
                             final_norm)
    y_prompt = _trunk(x_prompt, params, **TILES)
    y_sample = _trunk(x_sample, params, **TILES)
    return (y_prompt, y_sample)
```

```python
import functools
import math

import jax
import jax.numpy as jnp
from jax import lax
from jax.experimental import pallas as pl
from jax.experimental.pallas import tpu as pltpu

D_MODEL = 1024
GRID_W = 64
HEAD_DIM = 64
AXIAL_DIM = HEAD_DIM // 2
GQA_HEADS = 8
GQA_KV_HEADS = 2
DIFF_HEADS = 4
DIFF_V_DIM = 2 * HEAD_DIM
GQA_Q = GQA_HEADS * HEAD_DIM
GQA_KV = GQA_KV_HEADS * HEAD_DIM
DIFF_QK = DIFF_HEADS * 2 * HEAD_DIM
DIFF_V = DIFF_HEADS * DIFF_V_DIM
IN_WIDTH = GQA_Q + 2 * GQA_KV + 2 * DIFF_QK + DIFF_V
D_FF = 2816
ROPE_THETA = 10000.0
NORM_EPS = 1e-6
DIFF_NORM_EPS = 1e-5

LANES = 128
PROJ_ROW_CHAINS = 2
VMEM_LIMIT_BYTES = 56 * 1024 * 1024
Q_PRESCALE = (HEAD_DIM ** -0.5) * math.log2(math.e)

OFF_QG = 0
OFF_KG = GQA_Q
OFF_VG = OFF_KG + GQA_KV
OFF_QD = OFF_VG + GQA_KV
OFF_KD = OFF_QD + DIFF_QK
OFF_VD = OFF_KD + DIFF_QK

BF16 = jnp.bfloat16
F32 = jnp.float32


def _dot(a, b):
    return jnp.dot(a, b, preferred_element_type=F32)


def _rms(x, gain, eps):
    ms = jnp.mean(x * x, axis=-1, keepdims=True)
    return x * lax.rsqrt(ms + eps) * gain


def _proj_body(x_ref, an_ref, w_ref, qn_ref, kn_ref, ca_ref, sa_ref, cd_ref, sd_ref,
               qg_ref, kg_ref, vg_ref, qd_ref, kd_ref, vd_ref):
    tm = x_ref.shape[0] // PROJ_ROW_CHAINS
    lane = lax.broadcasted_iota(jnp.int32, (tm, LANES), 1)
    low_half = lane < HEAD_DIM
    r = (lax.broadcasted_iota(jnp.int32, (2 * LANES, LANES), 0) % HEAD_DIM) // (HEAD_DIM // 2)
    c = (lax.broadcasted_iota(jnp.int32, (2 * LANES, LANES), 1) % HEAD_DIM) // (HEAD_DIM // 2)
    blk = (r == c).astype(BF16)

    def head_norm(xc, gain):
        y = xc * xc
        hi = y.astype(BF16)
        lo = (y - hi.astype(F32)).astype(BF16)
        ss = _dot(jnp.concatenate([hi, lo], axis=1), blk)
        return xc * lax.rsqrt(ss * (1.0 / HEAD_DIM) + NORM_EPS) * gain

    def rope(xc, cos, sin_signed):
        return xc * cos + pltpu.roll(xc, HEAD_DIM, 1) * sin_signed

    one = jnp.ones((tm, LANES), F32)
    first_rows = (lax.broadcasted_iota(jnp.int32, (LANES, tm), 0) % HEAD_DIM) < (HEAD_DIM // 2)
    zero_t = jnp.zeros((LANES, tm), F32)

    def _proj_rows(rows):
        h = _rms(x_ref[rows, :], an_ref[...], NORM_EPS).astype(BF16)
        proj = _dot(h, w_ref[...])
        ca, sa, cd, sd = ca_ref[rows, :], sa_ref[rows, :], cd_ref[rows, :], sd_ref[rows, :]

        def project(offset, width):
            return proj[:, offset:offset + width]

        qg = project(OFF_QG, GQA_Q)
        for j in range(GQA_Q // LANES):
            xc = rope(head_norm(qg[:, j * LANES:(j + 1) * LANES], qn_ref[...]), ca, sa) * Q_PRESCALE
            qg_ref[rows, j * LANES:(j + 1) * LANES] = xc.astype(BF16)

        kv = project(OFF_KG, 2 * GQA_KV)
        kc = rope(head_norm(kv[:, :LANES], kn_ref[...]), ca, sa)
        to_first = pltpu.roll(kc, LANES - HEAD_DIM // 2, 1)
        to_second = pltpu.roll(kc, HEAD_DIM // 2, 1)
        kc_t, to_first_t, to_second_t = kc.T, to_first.T, to_second.T
        kg_ref[0 * LANES:1 * LANES, rows] = jnp.where(first_rows, kc_t, zero_t).astype(BF16)
        kg_ref[1 * LANES:2 * LANES, rows] = jnp.where(first_rows, zero_t, to_second_t).astype(BF16)
        kg_ref[2 * LANES:3 * LANES, rows] = jnp.where(first_rows, to_first_t, zero_t).astype(BF16)
        kg_ref[3 * LANES:4 * LANES, rows] = jnp.where(first_rows, zero_t, kc_t).astype(BF16)

        vc = kv[:, LANES:]
        vg_ref[rows, 0 * LANES:1 * LANES] = jnp.where(low_half, vc, one).astype(BF16)
        vg_ref[rows, 1 * LANES:2 * LANES] = jnp.where(low_half, pltpu.roll(vc, HEAD_DIM, 1), one).astype(BF16)

        qd = project(OFF_QD, DIFF_QK)
        for j in range(DIFF_HEADS):
            xc = rope(qd[:, j * LANES:(j + 1) * LANES], cd, sd) * Q_PRESCALE
            qd_ref[rows, j * LANES:(j + 1) * LANES] = xc.astype(BF16)
        kd = project(OFF_KD, DIFF_QK)
        for j in range(DIFF_HEADS):
            kc_t = rope(kd[:, j * LANES:(j + 1) * LANES], cd, sd).T
            kd_ref[(2 * j) * LANES:(2 * j + 1) * LANES, rows] = jnp.where(first_rows, kc_t, zero_t).astype(BF16)
            kd_ref[(2 * j + 1) * LANES:(2 * j + 2) * LANES, rows] = jnp.where(first_rows, zero_t, kc_t).astype(BF16)
        vd = project(OFF_VD, DIFF_V)
        for j in range(DIFF_HEADS):
            vd_ref[rows, (2 * j) * LANES:(2 * j + 1) * LANES] = vd[:, j * LANES:(j + 1) * LANES].astype(BF16)
            vd_ref[rows, (2 * j + 1) * LANES:(2 * j + 2) * LANES] = one.astype(BF16)

    for chain in range(PROJ_ROW_CHAINS):
        _proj_rows(pl.ds(chain * tm, tm))


def _proj_call(x2d, seq, layer, attn_norm, w_in, qn, kn, ca, sa, cd, sd, tm):
    tokens = x2d.shape[0]
    n_pos = seq // tm
    tok = lambda i: (i, 0)
    pos = lambda i: (i % n_pos, 0)
    resident = functools.partial(pl.BlockSpec, index_map=lambda i: (layer, 0, 0), pipeline_mode=pl.Buffered(1))
    batch = tokens // seq
    row_major = lambda w: (pl.BlockSpec((tm, w), tok), jax.ShapeDtypeStruct((tokens, w), BF16))
    transposed = lambda w: (pl.BlockSpec((None, w, tm), lambda i: (i // n_pos, 0, i % n_pos)),
                            jax.ShapeDtypeStruct((batch, w, seq), BF16))
    outs = (row_major(GQA_Q), transposed(4 * LANES), row_major(2 * LANES),
            row_major(DIFF_QK), transposed(2 * DIFF_QK), row_major(2 * DIFF_V))
    return pl.pallas_call(
        _proj_body,
        grid=(tokens // tm,),
        in_specs=[
            pl.BlockSpec((tm, D_MODEL), tok),
            resident((None, 1, D_MODEL)),
            resident((None, D_MODEL, IN_WIDTH)),
            resident((None, 1, LANES)),
            resident((None, 1, LANES)),
            pl.BlockSpec((tm, LANES), pos),
            pl.BlockSpec((tm, LANES), pos),
            pl.BlockSpec((tm, LANES), pos),
            pl.BlockSpec((tm, LANES), pos),
        ],
        out_specs=[spec for spec, _ in outs],
        out_shape=[shape for _, shape in outs],
        compiler_params=pltpu.CompilerParams(
            dimension_semantics=("parallel",), vmem_limit_bytes=VMEM_LIMIT_BYTES),
        name="proj",
    )(x2d, attn_norm, w_in, qn, kn, ca, sa, cd, sd)


UNROLLED_SCORES = 8 * 1024 * 1024


def _pipelined_rows(n_blocks, block_scores, scores, consume):
    per_trip = max(2, UNROLLED_SCORES // block_scores // 2 * 2)
    assert per_trip % 2 == 0 and n_blocks >= 2
    scores(0, 0)
    looped = ((n_blocks - 1) // per_trip) * per_trip

    def trip(t, carry):
        for u in range(per_trip):
            i = t * per_trip + u
            scores(i + 1, (u + 1) % 2)
            consume(i, u % 2)
        return carry

    lax.fori_loop(0, looped // per_trip, trip, 0)
    for i in range(looped, n_blocks):
        if i + 1 < n_blocks:
            scores(i + 1, (i + 1) % 2)
        consume(i, i % 2)


def _gqa_body(rows, q_ref, k_ref, v_ref, o_ref, s_scr0, s_scr1, p_scr0, p_scr1):
    s_scr = (s_scr0, s_scr1)
    p_scr = (p_scr0, p_scr1)
    seq = q_ref.shape[0]
    low_half = lax.broadcasted_iota(jnp.int32, (rows, LANES), 1) < HEAD_DIM

    def scores(i, slot):
        q = q_ref[pl.ds(pl.multiple_of(i * rows, rows), rows), :]
        q2 = jnp.concatenate([q[:, :LANES], q[:, LANES:]], axis=0)
        s_scr[slot][0] = _dot(q2, k_ref[:LANES, :])
        s_scr[slot][1] = _dot(q2, k_ref[LANES:, :])

    def consume(i, slot):
        for m in range(2):
            s = s_scr[slot][m]
            p_scr[slot][m * 2 * rows:(m + 1) * 2 * rows, :] = (
                jnp.exp2(s - jnp.max(s, axis=-1, keepdims=True)).astype(BF16))
        r = _dot(p_scr[slot][...], v_ref[...])
        r_lo, r_hi = r[:2 * rows], r[2 * rows:]
        cols = []
        for col in range(2):
            lo = r_lo[col * rows:(col + 1) * rows]
            hi = r_hi[col * rows:(col + 1) * rows]
            lo_sw = pltpu.roll(lo, HEAD_DIM, 1)
            hi_sw = pltpu.roll(hi, HEAD_DIM, 1)
            cols.append(jnp.where(low_half, lo / lo_sw, hi_sw / hi))
        out = jnp.concatenate(cols, axis=1)
        o_ref[pl.ds(pl.multiple_of(i * rows, rows), rows), :] = out.astype(o_ref.dtype)

    _pipelined_rows(seq // rows, (GQA_HEADS // GQA_KV_HEADS) * rows * seq, scores, consume)


def _gqa_call(qg, kg, vg, rows):
    batch, seq, _ = qg.shape
    return pl.pallas_call(
        functools.partial(_gqa_body, rows),
        grid=(batch, GQA_KV_HEADS),
        in_specs=[
            pl.BlockSpec((None, seq, 2 * LANES), lambda b, g: (b, 0, g)),
            pl.BlockSpec((None, 2 * LANES, seq), lambda b, g: (b, g, 0)),
            pl.BlockSpec((None, seq, LANES), lambda b, g: (b, 0, g)),
        ],
        out_specs=pl.BlockSpec((None, seq, 2 * LANES), lambda b, g: (b, 0, g)),
        out_shape=jax.ShapeDtypeStruct((batch, seq, GQA_Q), BF16),
        scratch_shapes=[pltpu.VMEM((2, 2 * rows, seq), F32)] * 2 + [pltpu.VMEM((4 * rows, seq), BF16)] * 2,
        compiler_params=pltpu.CompilerParams(
            dimension_semantics=("parallel", "parallel"),
            vmem_limit_bytes=VMEM_LIMIT_BYTES),
        name="gqa_attn",
    )(qg, kg, vg)


def _diff_body(lambda_init, rows, q_ref, k_ref, v_ref, lq1_ref, lk1_ref, lq2_ref, lk2_ref, sn_ref, o_ref,
               s_scr0, s_scr1, p_scr0, p_scr1):
    s_scr = (s_scr0, s_scr1)
    p_scr = (p_scr0, p_scr1)
    seq = q_ref.shape[0]
    lam = (jnp.exp(jnp.sum(lq1_ref[...] * lk1_ref[...], axis=-1, keepdims=True))
           - jnp.exp(jnp.sum(lq2_ref[...] * lk2_ref[...], axis=-1, keepdims=True)) + lambda_init)

    def scores(i, slot):
        q = q_ref[pl.ds(pl.multiple_of(i * rows, rows), rows), :]
        s_scr[slot][0] = _dot(q, k_ref[:LANES, :])
        s_scr[slot][1] = _dot(q, k_ref[LANES:, :])

    def consume(i, slot):
        for m in range(2):
            s = s_scr[slot][m]
            p_scr[slot][m * rows:(m + 1) * rows, :] = jnp.exp2(s - jnp.max(s, axis=-1, keepdims=True)).astype(BF16)
        r = _dot(p_scr[slot][...], v_ref[...])
        r1, r2 = r[:rows], r[rows:]
        o = r1[:, :LANES] / r1[:, LANES:] - lam * (r2[:, :LANES] / r2[:, LANES:])
        o = _rms(o, sn_ref[...], DIFF_NORM_EPS) * (1.0 - lambda_init)
        o_ref[pl.ds(pl.multiple_of(i * rows, rows), rows), :] = o.astype(o_ref.dtype)

    _pipelined_rows(seq // rows, 2 * rows * seq, scores, consume)


def _diff_call(qd, kd, vd, layer, lq1, lk1, lq2, lk2, sub_norm, lambda_init, rows):
    batch, seq, _ = qd.shape
    const = lambda b, h: (layer, 0, 0)
    return pl.pallas_call(
        functools.partial(_diff_body, lambda_init, rows),
        grid=(batch, DIFF_HEADS),
        in_specs=[
            pl.BlockSpec((None, seq, LANES), lambda b, h: (b, 0, h)),
            pl.BlockSpec((None, 2 * LANES, seq), lambda b, h: (b, h, 0)),
            pl.BlockSpec((None, seq, 2 * LANES), lambda b, h: (b, 0, h)),
            pl.BlockSpec((None, 1, HEAD_DIM), const),
            pl.BlockSpec((None, 1, HEAD_DIM), const),
            pl.BlockSpec((None, 1, HEAD_DIM), const),
            pl.BlockSpec((None, 1, HEAD_DIM), const),
            pl.BlockSpec((None, 1, DIFF_V_DIM), const),
        ],
        out_specs=pl.BlockSpec((None, seq, LANES), lambda b, h: (b, 0, h)),
        out_shape=jax.ShapeDtypeStruct((batch, seq, DIFF_V), BF16),
        scratch_shapes=[pltpu.VMEM((2, rows, seq), F32)] * 2 + [pltpu.VMEM((2 * rows, seq), BF16)] * 2,
        compiler_params=pltpu.CompilerParams(
            dimension_semantics=("parallel", "parallel"),
            vmem_limit_bytes=VMEM_LIMIT_BYTES),
        name="diff_attn",
    )(qd, kd, vd, lq1, lk1, lq2, lk2, sub_norm)


MXU_TILE = 256
FF_SPLITS = (0, 6 * MXU_TILE, D_FF)


def _mlp_body(apply_final_norm, x_ref, og_ref, od_ref, wo_ref, fn_ref, wgu_ref, wd_ref, final_ref, y_ref):
    x = x_ref[...] + (_dot(og_ref[...], wo_ref[:GQA_Q, :]) + _dot(od_ref[...], wo_ref[GQA_Q:, :]))
    h = _rms(x, fn_ref[...], NORM_EPS).astype(BF16)
    ffn = None
    for c0, c1 in zip(FF_SPLITS[:-1], FF_SPLITS[1:]):
        gate = _dot(h, wgu_ref[:, c0:c1])
        up = _dot(h, wgu_ref[:, D_FF + c0:D_FF + c1])
        act = (gate * jax.nn.sigmoid(gate) * up).astype(BF16)
        down = _dot(act, wd_ref[c0:c1, :])
        ffn = down if ffn is None else ffn + down
    x = x + ffn
    if apply_final_norm:
        x = _rms(x, final_ref[...], NORM_EPS)
    y_ref[...] = x


def _mlp_call(x2d, og, od, layer, w_out, ffn_norm, w_gate_up, w_down, final_norm, apply_final_norm, tm):
    tokens = x2d.shape[0]
    tok = lambda i: (i, 0)
    resident = functools.partial(pl.BlockSpec, index_map=lambda i: (layer, 0, 0), pipeline_mode=pl.Buffered(1))
    return pl.pallas_call(
        functools.partial(_mlp_body, apply_final_norm),
        grid=(tokens // tm,),
        in_specs=[
            pl.BlockSpec((tm, D_MODEL), tok),
            pl.BlockSpec((tm, GQA_Q), tok),
            pl.BlockSpec((tm, DIFF_V), tok),
            resident((None, GQA_Q + DIFF_V, D_MODEL)),
            resident((None, 1, D_MODEL)),
            resident((None, D_MODEL, 2 * D_FF)),
            resident((None, D_FF, D_MODEL)),
            pl.BlockSpec((1, D_MODEL), lambda i: (0, 0), pipeline_mode=pl.Buffered(1)),
        ],
        out_specs=pl.BlockSpec((tm, D_MODEL), tok),
        out_shape=jax.ShapeDtypeStruct((tokens, D_MODEL), F32),
        compiler_params=pltpu.CompilerParams(
            dimension_semantics=("parallel",), vmem_limit_bytes=VMEM_LIMIT_BYTES),
        name="mlp",
    )(x2d, og, od, w_out, ffn_norm, w_gate_up, w_down, final_norm)


def _permute_gqa_columns(a):
    lead = a.shape[:-1]
    a = a.reshape(lead + (a.shape[-1] // LANES, 2, 2, 2, AXIAL_DIM // 2))
    n = len(lead)
    a = a.transpose(tuple(range(n)) + (n, n + 3, n + 1, n + 2, n + 4))
    return a.reshape(lead + (-1,))


def _permute_diff_columns(a):
    lead = a.shape[:-1]
    a = a.reshape(lead + (a.shape[-1] // LANES, 2, 2, HEAD_DIM // 2))
    n = len(lead)
    a = a.transpose(tuple(range(n)) + (n, n + 2, n + 1, n + 3))
    return a.reshape(lead + (-1,))


def _permute_projection_columns(w_in):
    return jnp.concatenate([
        _permute_gqa_columns(w_in[..., OFF_QG:OFF_VG]), w_in[..., OFF_VG:OFF_QD],
        _permute_diff_columns(w_in[..., OFF_QD:OFF_VD]), w_in[..., OFF_VD:]], axis=-1)


def _rope_tables(seq):
    def inv_freq(dim):
        return ROPE_THETA ** (-jnp.arange(0, dim, 2, dtype=F32) / dim)

    t = lax.broadcasted_iota(jnp.int32, (seq, LANES), 0)
    lane = lax.broadcasted_iota(jnp.int32, (seq, LANES), 1)
    sign = jnp.where(lane < HEAD_DIM, -1.0, 1.0).astype(F32)
    pos_a = jnp.where((lane % AXIAL_DIM) < AXIAL_DIM // 2, t // GRID_W, t % GRID_W)
    ang_a = pos_a.astype(F32) * jnp.tile(inv_freq(AXIAL_DIM), LANES // (AXIAL_DIM // 2))[None, :]
    ang_d = t.astype(F32) * jnp.tile(inv_freq(HEAD_DIM), LANES // (HEAD_DIM // 2))[None, :]
    return jnp.cos(ang_a), jnp.sin(ang_a) * sign, jnp.cos(ang_d), jnp.sin(ang_d) * sign


def _trunk(x, params, proj_tm, mlp_tm, gqa_rows, diff_block_scores):
    batch, seq, _ = x.shape
    tokens = batch * seq
    diff_rows = diff_block_scores // (2 * seq)
    (w_in, w_out, attn_norm, qn, kn, lq1, lk1, lq2, lk2, sub_norm, ffn_norm, w_gate_up, w_down,
     final_norm) = params
    depth = w_in.shape[0]
    tables = _rope_tables(seq)
    x2d = x.reshape(tokens, D_MODEL)
    for l in range(depth):
        lambda_init = 0.8 - 0.6 * math.exp(-0.3 * l)
        qg, kg, vg, qd, kd, vd = _proj_call(x2d, seq, l, attn_norm, w_in, qn, kn, *tables, proj_tm)
        shape3 = lambda a: a.reshape(batch, seq, a.shape[-1])
        og = _gqa_call(shape3(qg), kg, shape3(vg), gqa_rows)
        od = _diff_call(shape3(qd), kd, shape3(vd), l, lq1, lk1, lq2, lk2, sub_norm, lambda_init,
                        diff_rows)
        x2d = _mlp_call(x2d, og.reshape(tokens, GQA_Q), od.reshape(tokens, DIFF_V), l, w_out, ffn_norm,
                        w_gate_up, w_down, final_norm, l == depth - 1, mlp_tm)
    return x2d.reshape(batch, seq, D_MODEL)


TILES = dict(proj_tm=1024, mlp_tm=512, gqa_rows=128, diff_block_scores=2 * 1024 * 1024)


def _prepare_params(w_in, w_out, attn_norm, gqa_q_norm, gqa_k_norm, diff_lambda_q1, diff_lambda_k1,
                    diff_lambda_q2, diff_lambda_k2, diff_sub_norm, ffn_norm, w_gate_up, w_down, final_norm):
    depth = w_in.shape[0]
    row = lambda a: a.reshape(depth, 1, a.shape[-1])
    tile2 = lambda a: _permute_gqa_columns(jnp.tile(a, (1, LANES // HEAD_DIM))).reshape(depth, 1, LANES)
    return (
        _permute_projection_columns(w_in).astype(BF16), w_out.astype(BF16), row(attn_norm),
        tile2(gqa_q_norm), tile2(gqa_k_norm),
        row(diff_lambda_q1), row(diff_lambda_k1), row(diff_lambda_q2), row(diff_lambda_k2),
        row(diff_sub_norm), row(ffn_norm), w_gate_up.astype(BF16), w_down.astype(BF16),
        final_norm.reshape(1, D_MODEL),
    )


def kernel(x_prompt, x_sample, w_in, w_out, attn_norm, gqa_q_norm, gqa_k_norm, diff_lambda_q1, diff_lambda_k1, diff_lambda_q2, diff_lambda_k2, diff_sub_norm, ffn_norm, w_gate_up, w_down, final_norm):
    params = _prepare_params(w_in, w_out, attn_norm, gqa_q_norm, gqa_k_norm, diff_lambda_q1, diff_lambda_k1,
                             diff_lambda_q2, diff_lambda_k2, diff_sub_norm, ffn_norm, w_gate_up, w_down,
                             final_norm)
    y_prompt = _trunk(x_prompt, params, **TILES)
    y_sample = _trunk(x_sample, params, **TILES)
    return (y_prompt, y_sample)
```

```python
import functools
import math

import jax
import jax.numpy as jnp
from jax import lax
from jax.experimental import pallas as pl
from jax.experimental.pallas import tpu as pltpu

D_MODEL = 1024
GRID_W = 64
HEAD_DIM = 64
AXIAL_DIM = HEAD_DIM // 2
GQA_HEADS = 8
GQA_KV_HEADS = 2
DIFF_HEADS = 4
DIFF_V_DIM = 2 * HEAD_DIM
GQA_Q = GQA_HEADS * HEAD_DIM
GQA_KV = GQA_KV_HEADS * HEAD_DIM
DIFF_QK = DIFF_HEADS * 2 * HEAD_DIM
DIFF_V = DIFF_HEADS * DIFF_V_DIM
IN_WIDTH = GQA_Q + 2 * GQA_KV + 2 * DIFF_QK + DIFF_V
D_FF = 2816
ROPE_THETA = 10000.0
NORM_EPS = 1e-6
DIFF_NORM_EPS = 1e-5

LANES = 128
PROJ_ROW_CHAINS = 2
VMEM_LIMIT_BYTES = 56 * 1024 * 1024
Q_PRESCALE = (HEAD_DIM ** -0.5) * math.log2(math.e)

OFF_QG = 0
OFF_KG = GQA_Q
OFF_VG = OFF_KG + GQA_KV
OFF_QD = OFF_VG + GQA_KV
OFF_KD = OFF_QD + DIFF_QK
OFF_VD = OFF_KD + DIFF_QK

BF16 = jnp.bfloat16
F32 = jnp.float32


def _dot(a, b):
    return jnp.dot(a, b, preferred_element_type=F32)


def _dot_nt(a, b):
    return lax.dot_general(a, b, (((1,), (1,)), ((), ())), preferred_element_type=F32)


def _rms(x, gain, eps):
    ms = jnp.mean(x * x, axis=-1, keepdims=True)
    return x * lax.rsqrt(ms + eps) * gain


def _proj_body(x_ref, an_ref, w_ref, qn_ref, kn_ref, ca_ref, sa_ref, cd_ref, sd_ref,
               qg_ref, kg_ref, vg_ref, qd_ref, kd_ref, vd_ref):
    tm = x_ref.shape[0] // PROJ_ROW_CHAINS
    lane = lax.broadcasted_iota(jnp.int32, (tm, LANES), 1)
    low_half = lane < HEAD_DIM
    first_slot = (lane % HEAD_DIM) < (HEAD_DIM // 2)
    r = (lax.broadcasted_iota(jnp.int32, (2 * LANES, LANES), 0) % HEAD_DIM) // (HEAD_DIM // 2)
    c = (lax.broadcasted_iota(jnp.int32, (2 * LANES, LANES), 1) % HEAD_DIM) // (HEAD_DIM // 2)
    blk = (r == c).astype(BF16)

    def head_norm(xc, gain):
        y = xc * xc
        hi = y.astype(BF16)
        lo = (y - hi.astype(F32)).astype(BF16)
        ss = _dot(jnp.concatenate([hi, lo], axis=1), blk)
        return xc * lax.rsqrt(ss * (1.0 / HEAD_DIM) + NORM_EPS) * gain

    def rope(xc, cos, sin_signed):
        return xc * cos + pltpu.roll(xc, HEAD_DIM, 1) * sin_signed

    zero = jnp.zeros((tm, LANES), F32)
    one = jnp.ones((tm, LANES), F32)

    def _proj_rows(rows):
        h = _rms(x_ref[rows, :], an_ref[...], NORM_EPS).astype(BF16)
        proj = _dot(h, w_ref[...])
        ca, sa, cd, sd = ca_ref[rows, :], sa_ref[rows, :], cd_ref[rows, :], sd_ref[rows, :]

        def project(offset, width):
            return proj[:, offset:offset + width]

        qg = project(OFF_QG, GQA_Q)
        for j in range(GQA_Q // LANES):
            xc = rope(head_norm(qg[:, j * LANES:(j + 1) * LANES], qn_ref[...]), ca, sa) * Q_PRESCALE
            qg_ref[rows, j * LANES:(j + 1) * LANES] = xc.astype(BF16)

        kv = project(OFF_KG, 2 * GQA_KV)
        kc = rope(head_norm(kv[:, :LANES], kn_ref[...]), ca, sa)
        to_first = pltpu.roll(kc, LANES - HEAD_DIM // 2, 1)
        to_second = pltpu.roll(kc, HEAD_DIM // 2, 1)
        kg_ref[rows, 0 * LANES:1 * LANES] = jnp.where(first_slot, kc, zero).astype(BF16)
        kg_ref[rows, 1 * LANES:2 * LANES] = jnp.where(first_slot, zero, to_second).astype(BF16)
        kg_ref[rows, 2 * LANES:3 * LANES] = jnp.where(first_slot, to_first, zero).astype(BF16)
        kg_ref[rows, 3 * LANES:4 * LANES] = jnp.where(first_slot, zero, kc).astype(BF16)

        vc = kv[:, LANES:]
        vg_ref[rows, 0 * LANES:1 * LANES] = jnp.where(low_half, vc, one).astype(BF16)
        vg_ref[rows, 1 * LANES:2 * LANES] = jnp.where(low_half, pltpu.roll(vc, HEAD_DIM, 1), one).astype(BF16)

        qd = project(OFF_QD, DIFF_QK)
        for j in range(DIFF_HEADS):
            xc = rope(qd[:, j * LANES:(j + 1) * LANES], cd, sd) * Q_PRESCALE
            qd_ref[rows, j * LANES:(j + 1) * LANES] = xc.astype(BF16)
        kd = project(OFF_KD, DIFF_QK)
        for j in range(DIFF_HEADS):
            kc = rope(kd[:, j * LANES:(j + 1) * LANES], cd, sd)
            kd_ref[rows, (2 * j) * LANES:(2 * j + 1) * LANES] = jnp.where(first_slot, kc, zero).astype(BF16)
            kd_ref[rows, (2 * j + 1) * LANES:(2 * j + 2) * LANES] = jnp.where(first_slot, zero, kc).astype(BF16)
        vd = project(OFF_VD, DIFF_V)
        for j in range(DIFF_HEADS):
            vd_ref[rows, (2 * j) * LANES:(2 * j + 1) * LANES] = vd[:, j * LANES:(j + 1) * LANES].astype(BF16)
            vd_ref[rows, (2 * j + 1) * LANES:(2 * j + 2) * LANES] = one.astype(BF16)

    for chain in range(PROJ_ROW_CHAINS):
        _proj_rows(pl.ds(chain * tm, tm))


def _proj_call(x2d, seq, layer, attn_norm, w_in, qn, kn, ca, sa, cd, sd, tm):
    tokens = x2d.shape[0]
    n_pos = seq // tm
    tok = lambda i: (i, 0)
    pos = lambda i: (i % n_pos, 0)
    resident = functools.partial(pl.BlockSpec, index_map=lambda i: (layer, 0, 0), pipeline_mode=pl.Buffered(1))
    widths = (GQA_Q, 4 * LANES, 2 * LANES, DIFF_QK, 2 * DIFF_QK, 2 * DIFF_V)
    return pl.pallas_call(
        _proj_body,
        grid=(tokens // tm,),
        in_specs=[
            pl.BlockSpec((tm, D_MODEL), tok),
            resident((None, 1, D_MODEL)),
            resident((None, D_MODEL, IN_WIDTH)),
            resident((None, 1, LANES)),
            resident((None, 1, LANES)),
            pl.BlockSpec((tm, LANES), pos),
            pl.BlockSpec((tm, LANES), pos),
            pl.BlockSpec((tm, LANES), pos),
            pl.BlockSpec((tm, LANES), pos),
        ],
        out_specs=[pl.BlockSpec((tm, w), tok) for w in widths],
        out_shape=[jax.ShapeDtypeStruct((tokens, w), BF16) for w in widths],
        compiler_params=pltpu.CompilerParams(
            dimension_semantics=("parallel",), vmem_limit_bytes=VMEM_LIMIT_BYTES),
        name="proj",
    )(x2d, attn_norm, w_in, qn, kn, ca, sa, cd, sd)


UNROLLED_SCORES = 8 * 1024 * 1024


def _pipelined_rows(n_blocks, block_scores, scores, consume):
    per_trip = max(2, UNROLLED_SCORES // block_scores // 2 * 2)
    assert per_trip % 2 == 0 and n_blocks >= 2
    scores(0, 0)
    looped = ((n_blocks - 1) // per_trip) * per_trip

    def trip(t, carry):
        for u in range(per_trip):
            i = t * per_trip + u
            scores(i + 1, (u + 1) % 2)
            consume(i, u % 2)
        return carry

    lax.fori_loop(0, looped // per_trip, trip, 0)
    for i in range(looped, n_blocks):
        if i + 1 < n_blocks:
            scores(i + 1, (i + 1) % 2)
        consume(i, i % 2)


def _gqa_body(rows, q_ref, k_ref, v_ref, o_ref, s_scr0, s_scr1, p_scr0, p_scr1):
    s_scr = (s_scr0, s_scr1)
    p_scr = (p_scr0, p_scr1)
    seq = q_ref.shape[0]
    low_half = lax.broadcasted_iota(jnp.int32, (rows, LANES), 1) < HEAD_DIM

    def scores(i, slot):
        q = q_ref[pl.ds(pl.multiple_of(i * rows, rows), rows), :]
        q2 = jnp.concatenate([q[:, :LANES], q[:, LANES:]], axis=0)
        s_scr[slot][0] = _dot_nt(q2, k_ref[:, :LANES])
        s_scr[slot][1] = _dot_nt(q2, k_ref[:, LANES:])

    def consume(i, slot):
        for m in range(2):
            s = s_scr[slot][m]
            p_scr[slot][m * 2 * rows:(m + 1) * 2 * rows, :] = (
                jnp.exp2((s - jnp.max(s, axis=-1, keepdims=True)).astype(BF16)))
        r = _dot(p_scr[slot][...], v_ref[...])
        r_lo, r_hi = r[:2 * rows], r[2 * rows:]
        cols = []
        for col in range(2):
            lo = r_lo[col * rows:(col + 1) * rows]
            hi = r_hi[col * rows:(col + 1) * rows]
            lo_sw = pltpu.roll(lo, HEAD_DIM, 1)
            hi_sw = pltpu.roll(hi, HEAD_DIM, 1)
            cols.append(jnp.where(low_half, lo / lo_sw, hi_sw / hi))
        out = jnp.concatenate(cols, axis=1)
        o_ref[pl.ds(pl.multiple_of(i * rows, rows), rows), :] = out.astype(o_ref.dtype)

    _pipelined_rows(seq // rows, (GQA_HEADS // GQA_KV_HEADS) * rows * seq, scores, consume)


def _gqa_call(qg, kg, vg, rows):
    batch, seq, _ = qg.shape
    return pl.pallas_call(
        functools.partial(_gqa_body, rows),
        grid=(batch, GQA_KV_HEADS),
        in_specs=[
            pl.BlockSpec((None, seq, 2 * LANES), lambda b, g: (b, 0, g)),
            pl.BlockSpec((None, seq, 2 * LANES), lambda b, g: (b, 0, g)),
            pl.BlockSpec((None, seq, LANES), lambda b, g: (b, 0, g)),
        ],
        out_specs=pl.BlockSpec((None, seq, 2 * LANES), lambda b, g: (b, 0, g)),
        out_shape=jax.ShapeDtypeStruct((batch, seq, GQA_Q), BF16),
        scratch_shapes=[pltpu.VMEM((2, 2 * rows, seq), F32)] * 2 + [pltpu.VMEM((4 * rows, seq), BF16)] * 2,
        compiler_params=pltpu.CompilerParams(
            dimension_semantics=("parallel", "parallel"),
            vmem_limit_bytes=VMEM_LIMIT_BYTES),
        name="gqa_attn",
    )(qg, kg, vg)


def _diff_body(lambda_init, rows, q_ref, k_ref, v_ref, lq1_ref, lk1_ref, lq2_ref, lk2_ref, sn_ref, o_ref,
               s_scr0, s_scr1, p_scr0, p_scr1):
    s_scr = (s_scr0, s_scr1)
    p_scr = (p_scr0, p_scr1)
    seq = q_ref.shape[0]
    lam = (jnp.exp(jnp.sum(lq1_ref[...] * lk1_ref[...], axis=-1, keepdims=True))
           - jnp.exp(jnp.sum(lq2_ref[...] * lk2_ref[...], axis=-1, keepdims=True)) + lambda_init)

    def scores(i, slot):
        q = q_ref[pl.ds(pl.multiple_of(i * rows, rows), rows), :]
        s_scr[slot][0] = _dot_nt(q, k_ref[:, :LANES])
        s_scr[slot][1] = _dot_nt(q, k_ref[:, LANES:])

    def consume(i, slot):
        for m in range(2):
            s = s_scr[slot][m]
            p_scr[slot][m * rows:(m + 1) * rows, :] = jnp.exp2((s - jnp.max(s, axis=-1, keepdims=True)).astype(BF16))
        r = _dot(p_scr[slot][...], v_ref[...])
        r1, r2 = r[:rows], r[rows:]
        o = r1[:, :LANES] / r1[:, LANES:] - lam * (r2[:, :LANES] / r2[:, LANES:])
        o = _rms(o, sn_ref[...], DIFF_NORM_EPS) * (1.0 - lambda_init)
        o_ref[pl.ds(pl.multiple_of(i * rows, rows), rows), :] = o.astype(o_ref.dtype)

    _pipelined_rows(seq // rows, 2 * rows * seq, scores, consume)


def _diff_call(qd, kd, vd, layer, lq1, lk1, lq2, lk2, sub_norm, lambda_init, rows):
    batch, seq, _ = qd.shape
    const = lambda b, h: (layer, 0, 0)
    return pl.pallas_call(
        functools.partial(_diff_body, lambda_init, rows),
        grid=(batch, DIFF_HEADS),
        in_specs=[
            pl.BlockSpec((None, seq, LANES), lambda b, h: (b, 0, h)),
            pl.BlockSpec((None, seq, 2 * LANES), lambda b, h: (b, 0, h)),
            pl.BlockSpec((None, seq, 2 * LANES), lambda b, h: (b, 0, h)),
            pl.BlockSpec((None, 1, HEAD_DIM), const),
            pl.BlockSpec((None, 1, HEAD_DIM), const),
            pl.BlockSpec((None, 1, HEAD_DIM), const),
            pl.BlockSpec((None, 1, HEAD_DIM), const),
            pl.BlockSpec((None, 1, DIFF_V_DIM), const),
        ],
        out_specs=pl.BlockSpec((None, seq, LANES), lambda b, h: (b, 0, h)),
        out_shape=jax.ShapeDtypeStruct((batch, seq, DIFF_V), BF16),
        scratch_shapes=[pltpu.VMEM((2, rows, seq), F32)] * 2 + [pltpu.VMEM((2 * rows, seq), BF16)] * 2,
        compiler_params=pltpu.CompilerParams(
            dimension_semantics=("parallel", "parallel"),
            vmem_limit_bytes=VMEM_LIMIT_BYTES),
        name="diff_attn",
    )(qd, kd, vd, lq1, lk1, lq2, lk2, sub_norm)


MXU_TILE = 256
FF_SPLITS = (0, 6 * MXU_TILE, D_FF)


def _mlp_body(apply_final_norm, x_ref, og_ref, od_ref, wo_ref, fn_ref, wgu_ref, wd_ref, final_ref, y_ref):
    x = x_ref[...] + (_dot(og_ref[...], wo_ref[:GQA_Q, :]) + _dot(od_ref[...], wo_ref[GQA_Q:, :]))
    h = _rms(x, fn_ref[...], NORM_EPS).astype(BF16)
    ffn = None
    for c0, c1 in zip(FF_SPLITS[:-1], FF_SPLITS[1:]):
        gate = _dot(h, wgu_ref[:, c0:c1])
        up = _dot(h, wgu_ref[:, D_FF + c0:D_FF + c1])
        act = (gate * jax.nn.sigmoid(gate) * up).astype(BF16)
        down = _dot(act, wd_ref[c0:c1, :])
        ffn = down if ffn is None else ffn + down
    x = x + ffn
    if apply_final_norm:
        x = _rms(x, final_ref[...], NORM_EPS)
    y_ref[...] = x


def _mlp_call(x2d, og, od, layer, w_out, ffn_norm, w_gate_up, w_down, final_norm, apply_final_norm, tm):
    tokens = x2d.shape[0]
    tok = lambda i: (i, 0)
    resident = functools.partial(pl.BlockSpec, index_map=lambda i: (layer, 0, 0), pipeline_mode=pl.Buffered(1))
    return pl.pallas_call(
        functools.partial(_mlp_body, apply_final_norm),
        grid=(tokens // tm,),
        in_specs=[
            pl.BlockSpec((tm, D_MODEL), tok),
            pl.BlockSpec((tm, GQA_Q), tok),
            pl.BlockSpec((tm, DIFF_V), tok),
            resident((None, GQA_Q + DIFF_V, D_MODEL)),
            resident((None, 1, D_MODEL)),
            resident((None, D_MODEL, 2 * D_FF)),
            resident((None, D_FF, D_MODEL)),
            pl.BlockSpec((1, D_MODEL), lambda i: (0, 0), pipeline_mode=pl.Buffered(1)),
        ],
        out_specs=pl.BlockSpec((tm, D_MODEL), tok),
        out_shape=jax.ShapeDtypeStruct((tokens, D_MODEL), F32),
        compiler_params=pltpu.CompilerParams(
            dimension_semantics=("parallel",), vmem_limit_bytes=VMEM_LIMIT_BYTES),
        name="mlp",
    )(x2d, og, od, w_out, ffn_norm, w_gate_up, w_down, final_norm)


def _permute_gqa_columns(a):
    lead = a.shape[:-1]
    a = a.reshape(lead + (a.shape[-1] // LANES, 2, 2, 2, AXIAL_DIM // 2))
    n = len(lead)
    a = a.transpose(tuple(range(n)) + (n, n + 3, n + 1, n + 2, n + 4))
    return a.reshape(lead + (-1,))


def _permute_diff_columns(a):
    lead = a.shape[:-1]
    a = a.reshape(lead + (a.shape[-1] // LANES, 2, 2, HEAD_DIM // 2))
    n = len(lead)
    a = a.transpose(tuple(range(n)) + (n, n + 2, n + 1, n + 3))
    return a.reshape(lead + (-1,))


def _permute_projection_columns(w_in):
    return jnp.concatenate([
        _permute_gqa_columns(w_in[..., OFF_QG:OFF_VG]), w_in[..., OFF_VG:OFF_QD],
        _permute_diff_columns(w_in[..., OFF_QD:OFF_VD]), w_in[..., OFF_VD:]], axis=-1)


def _rope_tables(seq):
    def inv_freq(dim):
        return ROPE_THETA ** (-jnp.arange(0, dim, 2, dtype=F32) / dim)

    t = lax.broadcasted_iota(jnp.int32, (seq, LANES), 0)
    lane = lax.broadcasted_iota(jnp.int32, (seq, LANES), 1)
    sign = jnp.where(lane < HEAD_DIM, -1.0, 1.0).astype(F32)
    pos_a = jnp.where((lane % AXIAL_DIM) < AXIAL_DIM // 2, t // GRID_W, t % GRID_W)
    ang_a = pos_a.astype(F32) * jnp.tile(inv_freq(AXIAL_DIM), LANES // (AXIAL_DIM // 2))[None, :]
    ang_d = t.astype(F32) * jnp.tile(inv_freq(HEAD_DIM), LANES // (HEAD_DIM // 2))[None, :]
    return jnp.cos(ang_a), jnp.sin(ang_a) * sign, jnp.cos(ang_d), jnp.sin(ang_d) * sign


def _trunk(x, params, proj_tm, mlp_tm, gqa_rows, diff_block_scores):
    batch, seq, _ = x.shape
    tokens = batch * seq
    diff_rows = diff_block_scores // (2 * seq)
    (w_in, w_out, attn_norm, qn, kn, lq1, lk1, lq2, lk2, sub_norm, ffn_norm, w_gate_up, w_down,
     final_norm) = params
    depth = w_in.shape[0]
    tables = _rope_tables(seq)
    x2d = x.reshape(tokens, D_MODEL)
    for l in range(depth):
        lambda_init = 0.8 - 0.6 * math.exp(-0.3 * l)
        qg, kg, vg, qd, kd, vd = _proj_call(x2d, seq, l, attn_norm, w_in, qn, kn, *tables, proj_tm)
        shape3 = lambda a: a.reshape(batch, seq, a.shape[-1])
        og = _gqa_call(shape3(qg), shape3(kg), shape3(vg), gqa_rows)
        od = _diff_call(shape3(qd), shape3(kd), shape3(vd), l, lq1, lk1, lq2, lk2, sub_norm, lambda_init,
                        diff_rows)
        x2d = _mlp_call(x2d, og.reshape(tokens, GQA_Q), od.reshape(tokens, DIFF_V), l, w_out, ffn_norm,
                        w_gate_up, w_down, final_norm, l == depth - 1, mlp_tm)
    return x2d.reshape(batch, seq, D_MODEL)


TILES = dict(proj_tm=1024, mlp_tm=512, gqa_rows=128, diff_block_scores=2 * 1024 * 1024)


def _prepare_params(w_in, w_out, attn_norm, gqa_q_norm, gqa_k_norm, diff_lambda_q1, diff_lambda_k1,
                    diff_lambda_q2, diff_lambda_k2, diff_sub_norm, ffn_norm, w_gate_up, w_down, final_norm):
    depth = w_in.shape[0]
    row = lambda a: a.reshape(depth, 1, a.shape[-1])
    tile2 = lambda a: _permute_gqa_columns(jnp.tile(a, (1, LANES // HEAD_DIM))).reshape(depth, 1, LANES)
    return (
        _permute_projection_columns(w_in).astype(BF16), w_out.astype(BF16), row(attn_norm),
        tile2(gqa_q_norm), tile2(gqa_k_norm),
        row(diff_lambda_q1), row(diff_lambda_k1), row(diff_lambda_q2), row(diff_lambda_k2),
        row(diff_sub_norm), row(ffn_norm), w_gate_up.astype(BF16), w_down.astype(BF16),
        final_norm.reshape(1, D_MODEL),
    )


def kernel(x_prompt, x_sample, w_in, w_out, attn_norm, gqa_q_norm, gqa_k_norm, diff_lambda_q1, diff_lambda_k1, diff_lambda_q2, diff_lambda_k2, diff_sub_norm, ffn_norm, w_gate_up, w_down, final_norm):
    params = _prepare_params(w_in, w_out, attn_norm, gqa_q_norm, gqa_k_norm, diff_lambda_q1, diff_lambda_k1,
                             diff_lambda_q2, diff_lambda_k2, diff_sub_norm, ffn_norm, w_gate_up, w_down,
                             final_norm)
    y_prompt = _trunk(x_prompt, params, **TILES)
    y_sample = _trunk(x_sample, params, **TILES)
    return (y_prompt, y_sample)
```

```python
import functools
import math

import jax
import jax.numpy as jnp
from jax import lax
from jax.experimental import pallas as pl
from jax.experimental.pallas import tpu as pltpu

D_MODEL = 1024
GRID_W = 64
HEAD_DIM = 64
AXIAL_DIM = HEAD_DIM // 2
GQA_HEADS = 8
GQA_KV_HEADS = 2
DIFF_HEADS = 4
DIFF_V_DIM = 2 * HEAD_DIM
GQA_Q = GQA_HEADS * HEAD_DIM
GQA_KV = GQA_KV_HEADS * HEAD_DIM
DIFF_QK = DIFF_HEADS * 2 * HEAD_DIM
DIFF_V = DIFF_HEADS * DIFF_V_DIM
IN_WIDTH = GQA_Q + 2 * GQA_KV + 2 * DIFF_QK + DIFF_V
D_FF = 2816
ROPE_THETA = 10000.0
NORM_EPS = 1e-6
DIFF_NORM_EPS = 1e-5

LANES = 128
PROJ_ROW_CHAINS = 2
VMEM_LIMIT_BYTES = 56 * 1024 * 1024
Q_PRESCALE = (HEAD_DIM ** -0.5) * math.log2(math.e)

OFF_QG = 0
OFF_KG = GQA_Q
OFF_VG = OFF_KG + GQA_KV
OFF_QD = OFF_VG + GQA_KV
OFF_KD = OFF_QD + DIFF_QK
OFF_VD = OFF_KD + DIFF_QK

BF16 = jnp.bfloat16
F32 = jnp.float32


def _dot(a, b):
    return jnp.dot(a, b, preferred_element_type=F32)


def _dot_nt(a, b):
    return lax.dot_general(a, b, (((1,), (1,)), ((), ())), preferred_element_type=F32)


def _rms(x, gain, eps):
    ms = jnp.mean(x * x, axis=-1, keepdims=True)
    return x * lax.rsqrt(ms + eps) * gain


def _proj_body(x_ref, an_ref, w_ref, qn_ref, kn_ref, ca_ref, sa_ref, cd_ref, sd_ref,
               qg_ref, kg_ref, vg_ref, qd_ref, kd_ref, vd_ref):
    tm = x_ref.shape[0] // PROJ_ROW_CHAINS
    lane = lax.broadcasted_iota(jnp.int32, (tm, LANES), 1)
    low_half = lane < HEAD_DIM
    first_slot = (lane % HEAD_DIM) < (HEAD_DIM // 2)
    r = (lax.broadcasted_iota(jnp.int32, (2 * LANES, LANES), 0) % HEAD_DIM) // (HEAD_DIM // 2)
    c = (lax.broadcasted_iota(jnp.int32, (2 * LANES, LANES), 1) % HEAD_DIM) // (HEAD_DIM // 2)
    blk = (r == c).astype(BF16)

    def head_norm(xc, gain):
        y = xc * xc
        hi = y.astype(BF16)
        lo = (y - hi.astype(F32)).astype(BF16)
        ss = _dot(jnp.concatenate([hi, lo], axis=1), blk)
        return xc * lax.rsqrt(ss * (1.0 / HEAD_DIM) + NORM_EPS) * gain

    def rope(xc, cos, sin_signed):
        return xc * cos + pltpu.roll(xc, HEAD_DIM, 1) * sin_signed

    zero = jnp.zeros((tm, LANES), F32)
    one = jnp.ones((tm, LANES), F32)

    def _proj_rows(rows):
        h = _rms(x_ref[rows, :], an_ref[...], NORM_EPS).astype(BF16)
        proj = _dot(h, w_ref[...])
        ca, sa, cd, sd = ca_ref[rows, :], sa_ref[rows, :], cd_ref[rows, :], sd_ref[rows, :]

        def project(offset, width):
            return proj[:, offset:offset + width]

        qg = project(OFF_QG, GQA_Q)
        for j in range(GQA_Q // LANES):
            xc = rope(head_norm(qg[:, j * LANES:(j + 1) * LANES], qn_ref[...]), ca, sa) * Q_PRESCALE
            qg_ref[rows, j * LANES:(j + 1) * LANES] = xc.astype(BF16)

        kv = project(OFF_KG, 2 * GQA_KV)
        kc = rope(head_norm(kv[:, :LANES], kn_ref[...]), ca, sa)
        to_first = pltpu.roll(kc, LANES - HEAD_DIM // 2, 1)
        to_second = pltpu.roll(kc, HEAD_DIM // 2, 1)
        kg_ref[rows, 0 * LANES:1 * LANES] = jnp.where(first_slot, kc, zero).astype(BF16)
        kg_ref[rows, 1 * LANES:2 * LANES] = jnp.where(first_slot, zero, to_second).astype(BF16)
        kg_ref[rows, 2 * LANES:3 * LANES] = jnp.where(first_slot, to_first, zero).astype(BF16)
        kg_ref[rows, 3 * LANES:4 * LANES] = jnp.where(first_slot, zero, kc).astype(BF16)

        vc = kv[:, LANES:]
        vg_ref[rows, 0 * LANES:1 * LANES] = jnp.where(low_half, vc, one).astype(BF16)
        vg_ref[rows, 1 * LANES:2 * LANES] = jnp.where(low_half, pltpu.roll(vc, HEAD_DIM, 1), one).astype(BF16)

        qd = project(OFF_QD, DIFF_QK)
        for j in range(DIFF_HEADS):
            xc = rope(qd[:, j * LANES:(j + 1) * LANES], cd, sd) * Q_PRESCALE
            qd_ref[rows, j * LANES:(j + 1) * LANES] = xc.astype(BF16)
        kd = project(OFF_KD, DIFF_QK)
        for j in range(DIFF_HEADS):
            kc = rope(kd[:, j * LANES:(j + 1) * LANES], cd, sd)
            kd_ref[rows, (2 * j) * LANES:(2 * j + 1) * LANES] = jnp.where(first_slot, kc, zero).astype(BF16)
            kd_ref[rows, (2 * j + 1) * LANES:(2 * j + 2) * LANES] = jnp.where(first_slot, zero, kc).astype(BF16)
        vd = project(OFF_VD, DIFF_V)
        for j in range(DIFF_HEADS):
            vd_ref[rows, (2 * j) * LANES:(2 * j + 1) * LANES] = vd[:, j * LANES:(j + 1) * LANES].astype(BF16)
            vd_ref[rows, (2 * j + 1) * LANES:(2 * j + 2) * LANES] = one.astype(BF16)

    for chain in range(PROJ_ROW_CHAINS):
        _proj_rows(pl.ds(chain * tm, tm))


def _proj_call(x2d, seq, layer, attn_norm, w_in, qn, kn, ca, sa, cd, sd, tm):
    tokens = x2d.shape[0]
    n_pos = seq // tm
    tok = lambda i: (i, 0)
    pos = lambda i: (i % n_pos, 0)
    resident = functools.partial(pl.BlockSpec, index_map=lambda i: (layer, 0, 0), pipeline_mode=pl.Buffered(1))
    widths = (GQA_Q, 4 * LANES, 2 * LANES, DIFF_QK, 2 * DIFF_QK, 2 * DIFF_V)
    return pl.pallas_call(
        _proj_body,
        grid=(tokens // tm,),
        in_specs=[
            pl.BlockSpec((tm, D_MODEL), tok),
            resident((None, 1, D_MODEL)),
            resident((None, D_MODEL, IN_WIDTH)),
            resident((None, 1, LANES)),
            resident((None, 1, LANES)),
            pl.BlockSpec((tm, LANES), pos),
            pl.BlockSpec((tm, LANES), pos),
            pl.BlockSpec((tm, LANES), pos),
            pl.BlockSpec((tm, LANES), pos),
        ],
        out_specs=[pl.BlockSpec((tm, w), tok) for w in widths],
        out_shape=[jax.ShapeDtypeStruct((tokens, w), BF16) for w in widths],
        compiler_params=pltpu.CompilerParams(
            dimension_semantics=("parallel",), vmem_limit_bytes=VMEM_LIMIT_BYTES),
        name="proj",
    )(x2d, attn_norm, w_in, qn, kn, ca, sa, cd, sd)


UNROLLED_SCORES = 8 * 1024 * 1024


def _pipelined_rows(n_blocks, block_scores, scores, consume):
    per_trip = max(2, UNROLLED_SCORES // block_scores // 2 * 2)
    assert per_trip % 2 == 0 and n_blocks >= 2
    scores(0, 0)
    looped = ((n_blocks - 1) // per_trip) * per_trip

    def trip(t, carry):
        for u in range(per_trip):
            i = t * per_trip + u
            scores(i + 1, (u + 1) % 2)
            consume(i, u % 2)
        return carry

    lax.fori_loop(0, looped // per_trip, trip, 0)
    for i in range(looped, n_blocks):
        if i + 1 < n_blocks:
            scores(i + 1, (i + 1) % 2)
        consume(i, i % 2)


def _gqa_body(rows, q_ref, k_ref, v_ref, o_ref, s_scr0, s_scr1, p_scr0, p_scr1):
    s_scr = (s_scr0, s_scr1)
    p_scr = (p_scr0, p_scr1)
    seq = q_ref.shape[0]
    low_half = lax.broadcasted_iota(jnp.int32, (rows, LANES), 1) < HEAD_DIM

    def scores(i, slot):
        q = q_ref[pl.ds(pl.multiple_of(i * rows, rows), rows), :]
        q2 = jnp.concatenate([q[:, :LANES], q[:, LANES:]], axis=0)
        s_scr[slot][0] = _dot_nt(q2, k_ref[:, :LANES])
        s_scr[slot][1] = _dot_nt(q2, k_ref[:, LANES:])

    def consume(i, slot):
        for m in range(2):
            s = s_scr[slot][m]
            p_scr[slot][m * 2 * rows:(m + 1) * 2 * rows, :] = (
                jnp.exp2(s - jnp.max(s, axis=-1, keepdims=True)).astype(BF16))
        r = _dot(p_scr[slot][...], v_ref[...])
        r_lo, r_hi = r[:2 * rows], r[2 * rows:]
        cols = []
        for col in range(2):
            lo = r_lo[col * rows:(col + 1) * rows]
            hi = r_hi[col * rows:(col + 1) * rows]
            lo_sw = pltpu.roll(lo, HEAD_DIM, 1)
            hi_sw = pltpu.roll(hi, HEAD_DIM, 1)
            cols.append(jnp.where(low_half, lo / lo_sw, hi_sw / hi))
        out = jnp.concatenate(cols, axis=1)
        o_ref[pl.ds(pl.multiple_of(i * rows, rows), rows), :] = out.astype(o_ref.dtype)

    _pipelined_rows(seq // rows, (GQA_HEADS // GQA_KV_HEADS) * rows * seq, scores, consume)


def _gqa_call(qg, kg, vg, rows):
    batch, seq, _ = qg.shape
    return pl.pallas_call(
        functools.partial(_gqa_body, rows),
        grid=(batch, GQA_KV_HEADS),
        in_specs=[
            pl.BlockSpec((None, seq, 2 * LANES), lambda b, g: (b, 0, g)),
            pl.BlockSpec((None, seq, 2 * LANES), lambda b, g: (b, 0, g)),
            pl.BlockSpec((None, seq, LANES), lambda b, g: (b, 0, g)),
        ],
        out_specs=pl.BlockSpec((None, seq, 2 * LANES), lambda b, g: (b, 0, g)),
        out_shape=jax.ShapeDtypeStruct((batch, seq, GQA_Q), BF16),
        scratch_shapes=[pltpu.VMEM((2, 2 * rows, seq), F32)] * 2 + [pltpu.VMEM((4 * rows, seq), BF16)] * 2,
        compiler_params=pltpu.CompilerParams(
            dimension_semantics=("parallel", "parallel"),
            vmem_limit_bytes=VMEM_LIMIT_BYTES),
        name="gqa_attn",
    )(qg, kg, vg)


def _diff_body(lambda_init, rows, q_ref, k_ref, v_ref, lq1_ref, lk1_ref, lq2_ref, lk2_ref, sn_ref, o_ref,
               s_scr0, s_scr1, p_scr0, p_scr1):
    s_scr = (s_scr0, s_scr1)
    p_scr = (p_scr0, p_scr1)
    seq = q_ref.shape[0]
    lam = (jnp.exp(jnp.sum(lq1_ref[...] * lk1_ref[...], axis=-1, keepdims=True))
           - jnp.exp(jnp.sum(lq2_ref[...] * lk2_ref[...], axis=-1, keepdims=True)) + lambda_init)

    def scores(i, slot):
        q = q_ref[pl.ds(pl.multiple_of(i * rows, rows), rows), :]
        s_scr[slot][0] = _dot_nt(q, k_ref[:, :LANES])
        s_scr[slot][1] = _dot_nt(q, k_ref[:, LANES:])

    def consume(i, slot):
        for m in range(2):
            s = s_scr[slot][m]
            p_scr[slot][m * rows:(m + 1) * rows, :] = jnp.exp2(s - jnp.max(s, axis=-1, keepdims=True)).astype(BF16)
        r = _dot(p_scr[slot][...], v_ref[...])
        r1, r2 = r[:rows], r[rows:]
        o = r1[:, :LANES] / r1[:, LANES:] - lam * (r2[:, :LANES] / r2[:, LANES:])
        o = _rms(o, sn_ref[...], DIFF_NORM_EPS) * (1.0 - lambda_init)
        o_ref[pl.ds(pl.multiple_of(i * rows, rows), rows), :] = o.astype(o_ref.dtype)

    _pipelined_rows(seq // rows, 2 * rows * seq, scores, consume)


def _diff_call(qd, kd, vd, layer, lq1, lk1, lq2, lk2, sub_norm, lambda_init, rows):
    batch, seq, _ = qd.shape
    const = lambda b, h: (layer, 0, 0)
    return pl.pallas_call(
        functools.partial(_diff_body, lambda_init, rows),
        grid=(batch, DIFF_HEADS),
        in_specs=[
            pl.BlockSpec((None, seq, LANES), lambda b, h: (b, 0, h)),
            pl.BlockSpec((None, seq, 2 * LANES), lambda b, h: (b, 0, h)),
            pl.BlockSpec((None, seq, 2 * LANES), lambda b, h: (b, 0, h)),
            pl.BlockSpec((None, 1, HEAD_DIM), const),
            pl.BlockSpec((None, 1, HEAD_DIM), const),
            pl.BlockSpec((None, 1, HEAD_DIM), const),
            pl.BlockSpec((None, 1, HEAD_DIM), const),
            pl.BlockSpec((None, 1, DIFF_V_DIM), const),
        ],
        out_specs=pl.BlockSpec((None, seq, LANES), lambda b, h: (b, 0, h)),
        out_shape=jax.ShapeDtypeStruct((batch, seq, DIFF_V), BF16),
        scratch_shapes=[pltpu.VMEM((2, rows, seq), F32)] * 2 + [pltpu.VMEM((2 * rows, seq), BF16)] * 2,
        compiler_params=pltpu.CompilerParams(
            dimension_semantics=("parallel", "parallel"),
            vmem_limit_bytes=VMEM_LIMIT_BYTES),
        name="diff_attn",
    )(qd, kd, vd, lq1, lk1, lq2, lk2, sub_norm)


MXU_TILE = 256
FF_SPLITS = (0, 6 * MXU_TILE, D_FF)


def _mlp_body(apply_final_norm, x_ref, og_ref, od_ref, wo_ref, fn_ref, wgu_ref, wd_ref, final_ref, y_ref):
    x = x_ref[...] + (_dot(og_ref[...], wo_ref[:GQA_Q, :]) + _dot(od_ref[...], wo_ref[GQA_Q:, :]))
    h = _rms(x, fn_ref[...], NORM_EPS).astype(BF16)
    gate_up = _dot(h, wgu_ref[...])
    gate, up = gate_up[:, :D_FF], gate_up[:, D_FF:]
    act = (gate * jax.nn.sigmoid(gate) * up).astype(BF16)
    x = x + _dot(act, wd_ref[...])
    if apply_final_norm:
        x = _rms(x, final_ref[...], NORM_EPS)
    y_ref[...] = x


def _mlp_call(x2d, og, od, layer, w_out, ffn_norm, w_gate_up, w_down, final_norm, apply_final_norm, tm):
    tokens = x2d.shape[0]
    tok = lambda i: (i, 0)
    resident = functools.partial(pl.BlockSpec, index_map=lambda i: (layer, 0, 0), pipeline_mode=pl.Buffered(1))
    return pl.pallas_call(
        functools.partial(_mlp_body, apply_final_norm),
        grid=(tokens // tm,),
        in_specs=[
            pl.BlockSpec((tm, D_MODEL), tok),
            pl.BlockSpec((tm, GQA_Q), tok),
            pl.BlockSpec((tm, DIFF_V), tok),
            resident((None, GQA_Q + DIFF_V, D_MODEL)),
            resident((None, 1, D_MODEL)),
            resident((None, D_MODEL, 2 * D_FF)),
            resident((None, D_FF, D_MODEL)),
            pl.BlockSpec((1, D_MODEL), lambda i: (0, 0), pipeline_mode=pl.Buffered(1)),
        ],
        out_specs=pl.BlockSpec((tm, D_MODEL), tok),
        out_shape=jax.ShapeDtypeStruct((tokens, D_MODEL), F32),
        compiler_params=pltpu.CompilerParams(
            dimension_semantics=("parallel",), vmem_limit_bytes=VMEM_LIMIT_BYTES),
        name="mlp",
    )(x2d, og, od, w_out, ffn_norm, w_gate_up, w_down, final_norm)


def _permute_gqa_columns(a):
    lead = a.shape[:-1]
    a = a.reshape(lead + (a.shape[-1] // LANES, 2, 2, 2, AXIAL_DIM // 2))
    n = len(lead)
    a = a.transpose(tuple(range(n)) + (n, n + 3, n + 1, n + 2, n + 4))
    return a.reshape(lead + (-1,))


def _permute_diff_columns(a):
    lead = a.shape[:-1]
    a = a.reshape(lead + (a.shape[-1] // LANES, 2, 2, HEAD_DIM // 2))
    n = len(lead)
    a = a.transpose(tuple(range(n)) + (n, n + 2, n + 1, n + 3))
    return a.reshape(lead + (-1,))


def _permute_projection_columns(w_in):
    return jnp.concatenate([
        _permute_gqa_columns(w_in[..., OFF_QG:OFF_VG]), w_in[..., OFF_VG:OFF_QD],
        _permute_diff_columns(w_in[..., OFF_QD:OFF_VD]), w_in[..., OFF_VD:]], axis=-1)


def _rope_tables(seq):
    def inv_freq(dim):
        return ROPE_THETA ** (-jnp.arange(0, dim, 2, dtype=F32) / dim)

    t = lax.broadcasted_iota(jnp.int32, (seq, LANES), 0)
    lane = lax.broadcasted_iota(jnp.int32, (seq, LANES), 1)
    sign = jnp.where(lane < HEAD_DIM, -1.0, 1.0).astype(F32)
    pos_a = jnp.where((lane % AXIAL_DIM) < AXIAL_DIM // 2, t // GRID_W, t % GRID_W)
    ang_a = pos_a.astype(F32) * jnp.tile(inv_freq(AXIAL_DIM), LANES // (AXIAL_DIM // 2))[None, :]
    ang_d = t.astype(F32) * jnp.tile(inv_freq(HEAD_DIM), LANES // (HEAD_DIM // 2))[None, :]
    return jnp.cos(ang_a), jnp.sin(ang_a) * sign, jnp.cos(ang_d), jnp.sin(ang_d) * sign


def _trunk(x, params, proj_tm, mlp_tm, gqa_rows, diff_block_scores):
    batch, seq, _ = x.shape
    tokens = batch * seq
    diff_rows = diff_block_scores // (2 * seq)
    (w_in, w_out, attn_norm, qn, kn, lq1, lk1, lq2, lk2, sub_norm, ffn_norm, w_gate_up, w_down,
     final_norm) = params
    depth = w_in.shape[0]
    tables = _rope_tables(seq)
    x2d = x.reshape(tokens, D_MODEL)
    for l in range(depth):
        lambda_init = 0.8 - 0.6 * math.exp(-0.3 * l)
        qg, kg, vg, qd, kd, vd = _proj_call(x2d, seq, l, attn_norm, w_in, qn, kn, *tables, proj_tm)
        shape3 = lambda a: a.reshape(batch, seq, a.shape[-1])
        og = _gqa_call(shape3(qg), shape3(kg), shape3(vg), gqa_rows)
        od = _diff_call(shape3(qd), shape3(kd), shape3(vd), l, lq1, lk1, lq2, lk2, sub_norm, lambda_init,
                        diff_rows)
        x2d = _mlp_call(x2d, og.reshape(tokens, GQA_Q), od.reshape(tokens, DIFF_V), l, w_out, ffn_norm,
                        w_gate_up, w_down, final_norm, l == depth - 1, mlp_tm)
    return x2d.reshape(batch, seq, D_MODEL)


TILES = dict(proj_tm=1024, mlp_tm=512, gqa_rows=128, diff_block_scores=2 * 1024 * 1024)


def _prepare_params(w_in, w_out, attn_norm, gqa_q_norm, gqa_k_norm, diff_lambda_q1, diff_lambda_k1,
                    diff_lambda_q2, diff_lambda_k2, diff_sub_norm, ffn_norm, w_gate_up, w_down, final_norm):
    depth = w_in.shape[0]
    row = lambda a: a.reshape(depth, 1, a.shape[-1])
    tile2 = lambda a: _permute_gqa_columns(jnp.tile(a, (1, LANES // HEAD_DIM))).reshape(depth, 1, LANES)
    return (
        _permute_projection_columns(w_in).astype(BF16), w_out.astype(BF16), row(attn_norm),
        tile2(gqa_q_norm), tile2(gqa_k_norm),
        row(diff_lambda_q1), row(diff_lambda_k1), row(diff_lambda_q2), row(diff_lambda_k2),
        row(diff_sub_norm), row(ffn_norm), w_gate_up.astype(BF16), w_down.astype(BF16),
        final_norm.reshape(1, D_MODEL),
    )


def kernel(x_prompt, x_sample, w_in, w_out, attn_norm, gqa_q_norm, gqa_k_norm, diff_lambda_q1, diff_lambda_k1, diff_lambda_q2, diff_lambda_k2, diff_sub_norm, ffn_norm, w_gate_up, w_down, final_norm):
    params = _prepare_params(w_in, w_out, attn_norm, gqa_q_norm, gqa_k_norm, diff_lambda_q1, diff_lambda_k1,
                             diff_lambda_q2, diff_lambda_k2, diff_sub_norm, ffn_norm, w_gate_up, w_down,
                             final_norm)
    y_prompt = _trunk(x_prompt, params, **TILES)
    y_sample = _trunk(x_sample, params, **TILES)
    return (y_prompt, y_sample)
```

```python
import functools
import math

import jax
import jax.numpy as jnp
from jax import lax
from jax.experimental import pallas as pl
from jax.experimental.pallas import tpu as pltpu

D_MODEL = 1024
GRID_W = 64
HEAD_DIM = 64
AXIAL_DIM = HEAD_DIM // 2
GQA_HEADS = 8
GQA_KV_HEADS = 2
DIFF_HEADS = 4
DIFF_V_DIM = 2 * HEAD_DIM
GQA_Q = GQA_HEADS * HEAD_DIM
GQA_KV = GQA_KV_HEADS * HEAD_DIM
DIFF_QK = DIFF_HEADS * 2 * HEAD_DIM
DIFF_V = DIFF_HEADS * DIFF_V_DIM
IN_WIDTH = GQA_Q + 2 * GQA_KV + 2 * DIFF_QK + DIFF_V
D_FF = 2816
ROPE_THETA = 10000.0
NORM_EPS = 1e-6
DIFF_NORM_EPS = 1e-5

LANES = 128
PROJ_ROW_CHAINS = 2
VMEM_LIMIT_BYTES = 56 * 1024 * 1024
Q_PRESCALE = (HEAD_DIM ** -0.5) * math.log2(math.e)

OFF_QG = 0
OFF_KG = GQA_Q
OFF_VG = OFF_KG + GQA_KV
OFF_QD = OFF_VG + GQA_KV
OFF_KD = OFF_QD + DIFF_QK
OFF_VD = OFF_KD + DIFF_QK

BF16 = jnp.bfloat16
F32 = jnp.float32


def _dot(a, b):
    return jnp.dot(a, b, preferred_element_type=F32)


def _dot_nt(a, b):
    return lax.dot_general(a, b, (((1,), (1,)), ((), ())), preferred_element_type=F32)


def _rms(x, gain, eps):
    ms = jnp.mean(x * x, axis=-1, keepdims=True)
    return x * lax.rsqrt(ms + eps) * gain


def _proj_body(x_ref, an_ref, w_ref, qn_ref, kn_ref, ca_ref, sa_ref, cd_ref, sd_ref,
               qg_ref, kg_ref, vg_ref, qd_ref, kd_ref, vd_ref):
    tm = x_ref.shape[0] // PROJ_ROW_CHAINS
    lane = lax.broadcasted_iota(jnp.int32, (tm, LANES), 1)
    low_half = lane < HEAD_DIM
    first_slot = (lane % HEAD_DIM) < (HEAD_DIM // 2)
    r = (lax.broadcasted_iota(jnp.int32, (2 * LANES, LANES), 0) % HEAD_DIM) // (HEAD_DIM // 2)
    c = (lax.broadcasted_iota(jnp.int32, (2 * LANES, LANES), 1) % HEAD_DIM) // (HEAD_DIM // 2)
    blk = (r == c).astype(BF16)

    def head_norm(xc, gain):
        y = xc * xc
        hi = y.astype(BF16)
        lo = (y - hi.astype(F32)).astype(BF16)
        ss = _dot(jnp.concatenate([hi, lo], axis=1), blk)
        return xc * lax.rsqrt(ss * (1.0 / HEAD_DIM) + NORM_EPS) * gain

    def rope(xc, cos, sin_signed):
        return xc * cos + pltpu.roll(xc, HEAD_DIM, 1) * sin_signed

    zero = jnp.zeros((tm, LANES), F32)
    one = jnp.ones((tm, LANES), F32)

    def _proj_rows(rows):
        h = _rms(x_ref[rows, :], an_ref[...], NORM_EPS).astype(BF16)
        proj = _dot(h, w_ref[...])
        ca, sa, cd, sd = ca_ref[rows, :], sa_ref[rows, :], cd_ref[rows, :], sd_ref[rows, :]

        def project(offset, width):
            return proj[:, offset:offset + width]

        qg = project(OFF_QG, GQA_Q)
        for j in range(GQA_Q // LANES):
            xc = rope(head_norm(qg[:, j * LANES:(j + 1) * LANES], qn_ref[...]), ca, sa) * Q_PRESCALE
            qg_ref[rows, j * LANES:(j + 1) * LANES] = xc.astype(BF16)

        kv = project(OFF_KG, 2 * GQA_KV)
        kc = rope(head_norm(kv[:, :LANES], kn_ref[...]), ca, sa)
        to_first = pltpu.roll(kc, LANES - HEAD_DIM // 2, 1)
        to_second = pltpu.roll(kc, HEAD_DIM // 2, 1)
        kg_ref[rows, 0 * LANES:1 * LANES] = jnp.where(first_slot, kc, zero).astype(BF16)
        kg_ref[rows, 1 * LANES:2 * LANES] = jnp.where(first_slot, zero, to_second).astype(BF16)
        kg_ref[rows, 2 * LANES:3 * LANES] = jnp.where(first_slot, to_first, zero).astype(BF16)
        kg_ref[rows, 3 * LANES:4 * LANES] = jnp.where(first_slot, zero, kc).astype(BF16)

        vc = kv[:, LANES:]
        vg_ref[rows, 0 * LANES:1 * LANES] = jnp.where(low_half, vc, one).astype(BF16)
        vg_ref[rows, 1 * LANES:2 * LANES] = jnp.where(low_half, pltpu.roll(vc, HEAD_DIM, 1), one).astype(BF16)

        qd = project(OFF_QD, DIFF_QK)
        for j in range(DIFF_HEADS):
            xc = rope(qd[:, j * LANES:(j + 1) * LANES], cd, sd) * Q_PRESCALE
            qd_ref[rows, j * LANES:(j + 1) * LANES] = xc.astype(BF16)
        kd = project(OFF_KD, DIFF_QK)
        for j in range(DIFF_HEADS):
            kc = rope(kd[:, j * LANES:(j + 1) * LANES], cd, sd)
            kd_ref[rows, (2 * j) * LANES:(2 * j + 1) * LANES] = jnp.where(first_slot, kc, zero).astype(BF16)
            kd_ref[rows, (2 * j + 1) * LANES:(2 * j + 2) * LANES] = jnp.where(first_slot, zero, kc).astype(BF16)
        vd = project(OFF_VD, DIFF_V)
        for j in range(DIFF_HEADS):
            vd_ref[rows, (2 * j) * LANES:(2 * j + 1) * LANES] = vd[:, j * LANES:(j + 1) * LANES].astype(BF16)
            vd_ref[rows, (2 * j + 1) * LANES:(2 * j + 2) * LANES] = one.astype(BF16)

    for chain in range(PROJ_ROW_CHAINS):
        _proj_rows(pl.ds(chain * tm, tm))


def _proj_call(x2d, seq, layer, attn_norm, w_in, qn, kn, ca, sa, cd, sd, tm):
    tokens = x2d.shape[0]
    n_pos = seq // tm
    tok = lambda i: (i, 0)
    pos = lambda i: (i % n_pos, 0)
    resident = functools.partial(pl.BlockSpec, index_map=lambda i: (layer, 0, 0), pipeline_mode=pl.Buffered(1))
    widths = (GQA_Q, 4 * LANES, 2 * LANES, DIFF_QK, 2 * DIFF_QK, 2 * DIFF_V)
    return pl.pallas_call(
        _proj_body,
        grid=(tokens // tm,),
        in_specs=[
            pl.BlockSpec((tm, D_MODEL), tok),
            resident((None, 1, D_MODEL)),
            resident((None, D_MODEL, IN_WIDTH)),
            resident((None, 1, LANES)),
            resident((None, 1, LANES)),
            pl.BlockSpec((tm, LANES), pos),
            pl.BlockSpec((tm, LANES), pos),
            pl.BlockSpec((tm, LANES), pos),
            pl.BlockSpec((tm, LANES), pos),
        ],
        out_specs=[pl.BlockSpec((tm, w), tok) for w in widths],
        out_shape=[jax.ShapeDtypeStruct((tokens, w), BF16) for w in widths],
        compiler_params=pltpu.CompilerParams(
            dimension_semantics=("parallel",), vmem_limit_bytes=VMEM_LIMIT_BYTES),
        name="proj",
    )(x2d, attn_norm, w_in, qn, kn, ca, sa, cd, sd)


UNROLLED_SCORES = 8 * 1024 * 1024


def _pipelined_rows(n_blocks, block_scores, scores, consume):
    per_trip = max(2, UNROLLED_SCORES // block_scores // 2 * 2)
    assert per_trip % 2 == 0 and n_blocks >= 2
    scores(0, 0)
    looped = ((n_blocks - 1) // per_trip) * per_trip

    def trip(t, carry):
        for u in range(per_trip):
            i = t * per_trip + u
            scores(i + 1, (u + 1) % 2)
            consume(i, u % 2)
        return carry

    lax.fori_loop(0, looped // per_trip, trip, 0)
    for i in range(looped, n_blocks):
        if i + 1 < n_blocks:
            scores(i + 1, (i + 1) % 2)
        consume(i, i % 2)


def _gqa_body(rows, q_ref, k_ref, v_ref, o_ref, s_scr0, s_scr1, p_scr0, p_scr1):
    s_scr = (s_scr0, s_scr1)
    p_scr = (p_scr0, p_scr1)
    seq = q_ref.shape[0]
    low_half = lax.broadcasted_iota(jnp.int32, (rows, LANES), 1) < HEAD_DIM

    def scores(i, slot):
        q = q_ref[pl.ds(pl.multiple_of(i * rows, rows), rows), :]
        q2 = jnp.concatenate([q[:, :LANES], q[:, LANES:]], axis=0)
        s_scr[slot][0] = _dot_nt(q2, k_ref[:, :LANES])
        s_scr[slot][1] = _dot_nt(q2, k_ref[:, LANES:])

    def consume(i, slot):
        for m in range(2):
            s = s_scr[slot][m]
            p_scr[slot][m * 2 * rows:(m + 1) * 2 * rows, :] = (
                jnp.exp2(s - jnp.max(s, axis=-1, keepdims=True)).astype(BF16))
        r = _dot(p_scr[slot][...], v_ref[...])
        r_lo, r_hi = r[:2 * rows], r[2 * rows:]
        cols = []
        for col in range(2):
            lo = r_lo[col * rows:(col + 1) * rows]
            hi = r_hi[col * rows:(col + 1) * rows]
            lo_sw = pltpu.roll(lo, HEAD_DIM, 1)
            hi_sw = pltpu.roll(hi, HEAD_DIM, 1)
            cols.append(jnp.where(low_half, lo / lo_sw, hi_sw / hi))
        out = jnp.concatenate(cols, axis=1)
        o_ref[pl.ds(pl.multiple_of(i * rows, rows), rows), :] = out.astype(o_ref.dtype)

    _pipelined_rows(seq // rows, (GQA_HEADS // GQA_KV_HEADS) * rows * seq, scores, consume)


def _gqa_call(qg, kg, vg, rows):
    batch, seq, _ = qg.shape
    return pl.pallas_call(
        functools.partial(_gqa_body, rows),
        grid=(batch, GQA_KV_HEADS),
        in_specs=[
            pl.BlockSpec((None, seq, 2 * LANES), lambda b, g: (b, 0, g)),
            pl.BlockSpec((None, seq, 2 * LANES), lambda b, g: (b, 0, g)),
            pl.BlockSpec((None, seq, LANES), lambda b, g: (b, 0, g)),
        ],
        out_specs=pl.BlockSpec((None, seq, 2 * LANES), lambda b, g: (b, 0, g)),
        out_shape=jax.ShapeDtypeStruct((batch, seq, GQA_Q), BF16),
        scratch_shapes=[pltpu.VMEM((2, 2 * rows, seq), F32)] * 2 + [pltpu.VMEM((4 * rows, seq), BF16)] * 2,
        compiler_params=pltpu.CompilerParams(
            dimension_semantics=("parallel", "parallel"),
            vmem_limit_bytes=VMEM_LIMIT_BYTES),
        name="gqa_attn",
    )(qg, kg, vg)


def _diff_body(lambda_init, rows, q_ref, k_ref, v_ref, lq1_ref, lk1_ref, lq2_ref, lk2_ref, sn_ref, o_ref,
               s_scr0, s_scr1, p_scr0, p_scr1):
    s_scr = (s_scr0, s_scr1)
    p_scr = (p_scr0, p_scr1)
    seq = q_ref.shape[0]
    lam = (jnp.exp(jnp.sum(lq1_ref[...] * lk1_ref[...], axis=-1, keepdims=True))
           - jnp.exp(jnp.sum(lq2_ref[...] * lk2_ref[...], axis=-1, keepdims=True)) + lambda_init)

    def scores(i, slot):
        q = q_ref[pl.ds(pl.multiple_of(i * rows, rows), rows), :]
        s_scr[slot][0] = _dot_nt(q, k_ref[:, :LANES])
        s_scr[slot][1] = _dot_nt(q, k_ref[:, LANES:])

    def consume(i, slot):
        for m in range(2):
            s = s_scr[slot][m]
            p_scr[slot][m * rows:(m + 1) * rows, :] = jnp.exp2(s - jnp.max(s, axis=-1, keepdims=True)).astype(BF16)
        r = _dot(p_scr[slot][...], v_ref[...])
        r1, r2 = r[:rows], r[rows:]
        o = r1[:, :LANES] / r1[:, LANES:] - lam * (r2[:, :LANES] / r2[:, LANES:])
        o = _rms(o, sn_ref[...], DIFF_NORM_EPS) * (1.0 - lambda_init)
        o_ref[pl.ds(pl.multiple_of(i * rows, rows), rows), :] = o.astype(o_ref.dtype)

    _pipelined_rows(seq // rows, 2 * rows * seq, scores, consume)


def _diff_call(qd, kd, vd, layer, lq1, lk1, lq2, lk2, sub_norm, lambda_init, rows):
    batch, seq, _ = qd.shape
    const = lambda b, h: (layer, 0, 0)
    return pl.pallas_call(
        functools.partial(_diff_body, lambda_init, rows),
        grid=(batch, DIFF_HEADS),
        in_specs=[
            pl.BlockSpec((None, seq, LANES), lambda b, h: (b, 0, h)),
            pl.BlockSpec((None, seq, 2 * LANES), lambda b, h: (b, 0, h)),
            pl.BlockSpec((None, seq, 2 * LANES), lambda b, h: (b, 0, h)),
            pl.BlockSpec((None, 1, HEAD_DIM), const),
            pl.BlockSpec((None, 1, HEAD_DIM), const),
            pl.BlockSpec((None, 1, HEAD_DIM), const),
            pl.BlockSpec((None, 1, HEAD_DIM), const),
            pl.BlockSpec((None, 1, DIFF_V_DIM), const),
        ],
        out_specs=pl.BlockSpec((None, seq, LANES), lambda b, h: (b, 0, h)),
        out_shape=jax.ShapeDtypeStruct((batch, seq, DIFF_V), BF16),
        scratch_shapes=[pltpu.VMEM((2, rows, seq), F32)] * 2 + [pltpu.VMEM((2 * rows, seq), BF16)] * 2,
        compiler_params=pltpu.CompilerParams(
            dimension_semantics=("parallel", "parallel"),
            vmem_limit_bytes=VMEM_LIMIT_BYTES),
        name="diff_attn",
    )(qd, kd, vd, lq1, lk1, lq2, lk2, sub_norm)


def _mlp_body(apply_final_norm, x_ref, og_ref, od_ref, wo_ref, fn_ref, wgu_ref, wd_ref, final_ref, y_ref):
    x = x_ref[...] + (_dot(og_ref[...], wo_ref[:GQA_Q, :]) + _dot(od_ref[...], wo_ref[GQA_Q:, :]))
    h = _rms(x, fn_ref[...], NORM_EPS).astype(BF16)
    gate_up = _dot(h, wgu_ref[...])
    gate, up = gate_up[:, :D_FF], gate_up[:, D_FF:]
    act = (gate * jax.nn.sigmoid(gate) * up).astype(BF16)
    x = x + _dot(act, wd_ref[...])
    if apply_final_norm:
        x = _rms(x, final_ref[...], NORM_EPS)
    y_ref[...] = x


def _mlp_call(x2d, og, od, layer, w_out, ffn_norm, w_gate_up, w_down, final_norm, apply_final_norm, tm):
    tokens = x2d.shape[0]
    tok = lambda i: (i, 0)
    resident = functools.partial(pl.BlockSpec, index_map=lambda i: (layer, 0, 0), pipeline_mode=pl.Buffered(1))
    return pl.pallas_call(
        functools.partial(_mlp_body, apply_final_norm),
        grid=(tokens // tm,),
        in_specs=[
            pl.BlockSpec((tm, D_MODEL), tok),
            pl.BlockSpec((tm, GQA_Q), tok),
            pl.BlockSpec((tm, DIFF_V), tok),
            resident((None, GQA_Q + DIFF_V, D_MODEL)),
            resident((None, 1, D_MODEL)),
            resident((None, D_MODEL, 2 * D_FF)),
            resident((None, D_FF, D_MODEL)),
            pl.BlockSpec((1, D_MODEL), lambda i: (0, 0), pipeline_mode=pl.Buffered(1)),
        ],
        out_specs=pl.BlockSpec((tm, D_MODEL), tok),
        out_shape=jax.ShapeDtypeStruct((tokens, D_MODEL), F32),
        compiler_params=pltpu.CompilerParams(
            dimension_semantics=("parallel",), vmem_limit_bytes=VMEM_LIMIT_BYTES),
        name="mlp",
    )(x2d, og, od, w_out, ffn_norm, w_gate_up, w_down, final_norm)


def _permute_gqa_columns(a):
    lead = a.shape[:-1]
    a = a.reshape(lead + (a.shape[-1] // LANES, 2, 2, 2, AXIAL_DIM // 2))
    n = len(lead)
    a = a.transpose(tuple(range(n)) + (n, n + 3, n + 1, n + 2, n + 4))
    return a.reshape(lead + (-1,))


def _permute_diff_columns(a):
    lead = a.shape[:-1]
    a = a.reshape(lead + (a.shape[-1] // LANES, 2, 2, HEAD_DIM // 2))
    n = len(lead)
    a = a.transpose(tuple(range(n)) + (n, n + 2, n + 1, n + 3))
    return a.reshape(lead + (-1,))


def _permute_projection_columns(w_in):
    return jnp.concatenate([
        _permute_gqa_columns(w_in[..., OFF_QG:OFF_VG]), w_in[..., OFF_VG:OFF_QD],
        _permute_diff_columns(w_in[..., OFF_QD:OFF_VD]), w_in[..., OFF_VD:]], axis=-1)


def _rope_tables(seq):
    def inv_freq(dim):
        return ROPE_THETA ** (-jnp.arange(0, dim, 2, dtype=F32) / dim)

    t = lax.broadcasted_iota(jnp.int32, (seq, LANES), 0)
    lane = lax.broadcasted_iota(jnp.int32, (seq, LANES), 1)
    sign = jnp.where(lane < HEAD_DIM, -1.0, 1.0).astype(F32)
    pos_a = jnp.where((lane % AXIAL_DIM) < AXIAL_DIM // 2, t // GRID_W, t % GRID_W)
    ang_a = pos_a.astype(F32) * jnp.tile(inv_freq(AXIAL_DIM), LANES // (AXIAL_DIM // 2))[None, :]
    ang_d = t.astype(F32) * jnp.tile(inv_freq(HEAD_DIM), LANES // (HEAD_DIM // 2))[None, :]
    return jnp.cos(ang_a), jnp.sin(ang_a) * sign, jnp.cos(ang_d), jnp.sin(ang_d) * sign


def _trunk(x, params, proj_tm, mlp_tm, gqa_rows, diff_block_scores):
    batch, seq, _ = x.shape
    tokens = batch * seq
    diff_rows = diff_block_scores // (2 * seq)
    (w_in, w_out, attn_norm, qn, kn, lq1, lk1, lq2, lk2, sub_norm, ffn_norm, w_gate_up, w_down,
     final_norm) = params
    depth = w_in.shape[0]
    tables = _rope_tables(seq)
    x2d = x.reshape(tokens, D_MODEL)
    for l in range(depth):
        lambda_init = 0.8 - 0.6 * math.exp(-0.3 * l)
        qg, kg, vg, qd, kd, vd = _proj_call(x2d, seq, l, attn_norm, w_in, qn, kn, *tables, proj_tm)
        shape3 = lambda a: a.reshape(batch, seq, a.shape[-1])
        og = _gqa_call(shape3(qg), shape3(kg), shape3(vg), gqa_rows)
        od = _diff_call(shape3(qd), shape3(kd), shape3(vd), l, lq1, lk1, lq2, lk2, sub_norm, lambda_init,
                        diff_rows)
        x2d = _mlp_call(x2d, og.reshape(tokens, GQA_Q), od.reshape(tokens, DIFF_V), l, w_out, ffn_norm,
                        w_gate_up, w_down, final_norm, l == depth - 1, mlp_tm)
    return x2d.reshape(batch, seq, D_MODEL)


TILES = dict(proj_tm=1024, mlp_tm=512, gqa_rows=128, diff_block_scores=2 * 1024 * 1024)


def _prepare_params(w_in, w_out, attn_norm, gqa_q_norm, gqa_k_norm, diff_lambda_q1, diff_lambda_k1,
                    diff_lambda_q2, diff_lambda_k2, diff_sub_norm, ffn_norm, w_gate_up, w_down, final_norm):
    depth = w_in.shape[0]
    row = lambda a: a.reshape(depth, 1, a.shape[-1])
    tile2 = lambda a: _permute_gqa_columns(jnp.tile(a, (1, LANES // HEAD_DIM))).reshape(depth, 1, LANES)
    return (
        _permute_projection_columns(w_in).astype(BF16), w_out.astype(BF16), row(attn_norm),
        tile2(gqa_q_norm), tile2(gqa_k_norm),
        row(diff_lambda_q1), row(diff_lambda_k1), row(diff_lambda_q2), row(diff_lambda_k2),
        row(diff_sub_norm), row(ffn_norm), w_gate_up.astype(BF16), w_down.astype(BF16),
        final_norm.reshape(1, D_MODEL),
    )


def kernel(x_prompt, x_sample, w_in, w_out, attn_norm, gqa_q_norm, gqa_k_norm, diff_lambda_q1, diff_lambda_k1, diff_lambda_q2, diff_lambda_k2, diff_sub_norm, ffn_norm, w_gate_up, w_down, final_norm):
    params = _prepare_params(w_in, w_out, attn_norm, gqa_q_norm, gqa_k_norm, diff_lambda_q1, diff_lambda_k1,
                             diff_lambda_q2, diff_lambda_k2, diff_sub_norm, ffn_norm, w_gate_up, w_down,
                             final_norm)
    y_prompt = _trunk(x_prompt, params, **TILES)
    y_sample = _trunk(x_sample, params, **TILES)
    return (y_prompt, y_sample)
```
